```python
import jax, jax.numpy as jnp
from jax import lax
import numpy as np

D_MODEL = 2048
BATCH = 4
SEQ = 4096
DEPTH = 4

CHUNK = 64
N_MIXERS = 2
N_LAYERS_A = (DEPTH + 1) // 2
N_LAYERS_B = DEPTH // 2
CONV_A_WIDTH = 31
CONV_B_WIDTH = 3
D_FF = 4 * D_MODEL
PLE_DIM = 256
EPS = 1e-6

kernel_name = "hybrid_conformer_shortconv_trunk"


def rmsnorm(x, g):
    xf = x.astype(jnp.float32)
    r = lax.rsqrt(jnp.mean(xf * xf, axis=-1, keepdims=True) + EPS)
    return (xf * r).astype(x.dtype) * g


def causal_depthwise_conv(x, w):
    k_width, channels = w.shape
    return lax.conv_general_dilated(
        x, w[:, None, :],
        window_strides=(1,),
        padding=[(k_width - 1, 0)],
        dimension_numbers=("NWC", "WIO", "NWC"),
        feature_group_count=channels)


def conformer_conv_module(u, w_pw1, b_pw1, w_dw, b_dw, g_norm, w_pw2, b_pw2):
    a = jnp.einsum("bsd,de->bse", u, w_pw1) + b_pw1
    val, gate = jnp.split(a, 2, axis=-1)
    v = val * jax.nn.sigmoid(gate)
    v = causal_depthwise_conv(v, w_dw) + b_dw
    v = rmsnorm(v, g_norm)
    v = jax.nn.silu(v)
    return jnp.einsum("bsd,de->bse", v, w_pw2) + b_pw2


def short_gated_conv(u, w_in, w_conv, w_out):
    bcv = jnp.einsum("bsd,de->bse", u, w_in)
    gate_b, gate_c, v = jnp.split(bcv, 3, axis=-1)
    y = gate_b * causal_depthwise_conv(gate_c * v, w_conv)
    return jnp.einsum("bsd,de->bse", y, w_out)


def squared_relu_mlp(u, w1, w2):
    hdn = jnp.square(jax.nn.relu(jnp.einsum("bsd,df->bsf", u, w1)))
    return jnp.einsum("bsf,fd->bsd", hdn, w2)


def per_layer_embedding(h, p_i, g_norm, w_proj, w_gate):
    e = jnp.einsum("bsk,kd->bsd", p_i, w_proj)
    g = jax.nn.sigmoid(jnp.einsum("bsd,de->bse", rmsnorm(h, g_norm), w_gate))
    return g * e


def setup_inputs(seed: int = 0) -> dict:
    key = jax.random.key(seed)
    ks = jax.random.split(key, 24)

    def nrm(k, shape, scale):
        return jax.random.normal(k, shape, jnp.float32) * scale

    def gain(k, shape):
        return 1.0 + 0.05 * jax.random.normal(k, shape, jnp.float32)

    D = D_MODEL
    return {
        "x": nrm(ks[0], (BATCH, SEQ, D), 1.0),
        "p": nrm(ks[1], (DEPTH, BATCH, SEQ, PLE_DIM), 1.0),
        "norm_mix": gain(ks[2], (DEPTH, D)),
        "norm_mlp": gain(ks[3], (DEPTH, D)),
        "norm_ple": gain(ks[4], (DEPTH, D)),
        "cf_w_pw1": nrm(ks[5], (N_LAYERS_A, D, 2 * D), D ** -0.5),
        "cf_b_pw1": nrm(ks[6], (N_LAYERS_A, 2 * D), 0.02),
        "cf_w_dw": nrm(ks[7], (N_LAYERS_A, CONV_A_WIDTH, D), CONV_A_WIDTH ** -0.5),
        "cf_b_dw": nrm(ks[8], (N_LAYERS_A, D), 0.02),
        "cf_norm": gain(ks[9], (N_LAYERS_A, D)),
        "cf_w_pw2": nrm(ks[10], (N_LAYERS_A, D, D), D ** -0.5),
        "cf_b_pw2": nrm(ks[11], (N_LAYERS_A, D), 0.02),
        "sc_w_in": nrm(ks[12], (N_LAYERS_B, D, 3 * D), D ** -0.5),
        "sc_w_conv": nrm(ks[13], (N_LAYERS_B, CONV_B_WIDTH, D), CONV_B_WIDTH ** -0.5),
        "sc_w_out": nrm(ks[14], (N_LAYERS_B, D, D), D ** -0.5),
        "mlp_w1": nrm(ks[15], (DEPTH, D, D_FF), D ** -0.5),
        "mlp_w2": nrm(ks[16], (DEPTH, D_FF, D), D_FF ** -0.5),
        "ple_w_proj": nrm(ks[17], (DEPTH, PLE_DIM, D), PLE_DIM ** -0.5),
        "ple_w_gate": nrm(ks[18], (DEPTH, D, D), D ** -0.5),
        "norm_final": gain(ks[19], (D,)),
    }


def reference(x, p, norm_mix, norm_mlp, norm_ple,
              cf_w_pw1, cf_b_pw1, cf_w_dw, cf_b_dw, cf_norm, cf_w_pw2, cf_b_pw2,
              sc_w_in, sc_w_conv, sc_w_out,
              mlp_w1, mlp_w2, ple_w_proj, ple_w_gate, norm_final):
    h = x
    for i in range(DEPTH):
        j = i // N_MIXERS
        u = rmsnorm(h, norm_mix[i])
        if i % N_MIXERS == 0:
            m = conformer_conv_module(u, cf_w_pw1[j], cf_b_pw1[j], cf_w_dw[j], cf_b_dw[j],
                                      cf_norm[j], cf_w_pw2[j], cf_b_pw2[j])
        else:
            m = short_gated_conv(u, sc_w_in[j], sc_w_conv[j], sc_w_out[j])
        h = h + m
        h = h + squared_relu_mlp(rmsnorm(h, norm_mlp[i]), mlp_w1[i], mlp_w2[i])
        h = h + per_layer_embedding(h, p[i], norm_ple[i], ple_w_proj[i], ple_w_gate[i])
    return rmsnorm(h, norm_final)
```

```python
import functools

import jax
import jax.numpy as jnp
from jax.experimental import pallas as pl
from jax.experimental.pallas import tpu as pltpu

EPS = 1e-6
LANES = 128
CONV_A_WIDTH = 31
CONV_B_WIDTH = 3
CONV_A_HALO = 32
CONV_B_HALO = 8
CONV_ROW_CHUNK = 64
NORM_ROW_CHUNK = 128
V7X_VMEM_LIMIT_BYTES = 56 * 1024 * 1024

F32 = jnp.float32
BF16 = jnp.bfloat16


def _params(n_axes):
    sem = ("parallel",) + ("arbitrary",) * (n_axes - 1)
    return pltpu.CompilerParams(dimension_semantics=sem,
                                vmem_limit_bytes=V7X_VMEM_LIMIT_BYTES)


def _rmsnorm(x, g):
    r = jax.lax.rsqrt(jnp.mean(x * x, axis=-1, keepdims=True) + EPS)
    return (x * r) * g


def _dot(a, b):
    return jnp.dot(a, b, preferred_element_type=F32)


def _store_lane_blocks(o_ref, val):
    for c in range(o_ref.shape[0]):
        o_ref[c] = val[:, c * LANES:(c + 1) * LANES]


def _fill_history(xe_ref, cur_ref, prev_ref, halo, tm, seq):
    first = (pl.program_id(0) * tm) % seq == 0
    keep = jnp.where(first, 0.0, 1.0).astype(F32)
    xe_ref[:, pl.ds(0, halo), :] = prev_ref[...] * keep
    xe_ref[:, pl.ds(halo, tm), :] = cur_ref[...]


def _conv_taps(xe_ref, w_ref, cb, r0, rows, halo, width):
    base = halo - (width - 1)
    acc = None
    for k in range(width):
        term = xe_ref[cb, pl.ds(r0 + base + k, rows), :] * w_ref[cb, pl.ds(k, 1), :]
        acc = term if acc is None else acc + term
    return acc


def _mix_a_in_kernel(x_ref, g_ref, wv_ref, wg_ref, bv_ref, bg_ref, v_ref, u_ref):
    @pl.when(pl.program_id(1) == 0)
    def _():
        u_ref[...] = _rmsnorm(x_ref[...], g_ref[...]).astype(BF16)

    u = u_ref[...]
    val = _dot(u, wv_ref[...]) + bv_ref[...]
    gate = _dot(u, wg_ref[...]) + bg_ref[...]
    _store_lane_blocks(v_ref, val * jax.nn.sigmoid(gate))


def _mix_a_in(h, g, w1, b1, *, tm, tn):
    t, d = h.shape
    nj = d // tn
    return pl.pallas_call(
        _mix_a_in_kernel,
        grid=(t // tm, nj),
        in_specs=[
            pl.BlockSpec((tm, d), lambda i, j: (i, 0)),
            pl.BlockSpec((1, d), lambda i, j: (0, 0)),
            pl.BlockSpec((d, tn), lambda i, j: (0, j)),
            pl.BlockSpec((d, tn), lambda i, j: (0, j + nj)),
            pl.BlockSpec((1, tn), lambda i, j: (0, j)),
            pl.BlockSpec((1, tn), lambda i, j: (0, j + nj)),
        ],
        out_specs=pl.BlockSpec((tn // LANES, tm, LANES), lambda i, j: (j, i, 0)),
        out_shape=jax.ShapeDtypeStruct((d // LANES, t, LANES), F32),
        scratch_shapes=[pltpu.VMEM((tm, d), BF16)],
        compiler_params=_params(2),
        name="mix_a_in",
    )(h, g, w1, w1, b1, b1)


def _mix_a_out_kernel(v_ref, vprev_ref, h_ref, wdw_ref, bdw_ref, gn_ref, w2_ref, b2_ref,
                      o_ref, xe_ref, c_ref, y_ref, *, tm, seq):
    halo, width, rows = CONV_A_HALO, CONV_A_WIDTH, CONV_ROW_CHUNK
    nblk = v_ref.shape[0]
    d = nblk * LANES
    _fill_history(xe_ref, v_ref, vprev_ref, halo, tm, seq)

    def lane_block(cb, carry):
        for r0 in range(0, tm, rows):
            c = _conv_taps(xe_ref, wdw_ref, cb, r0, rows, halo, width) + bdw_ref[cb]
            c_ref[cb, pl.ds(r0, rows), :] = c
        return carry

    jax.lax.fori_loop(0, nblk, lane_block, 0)

    for r0 in range(0, tm, NORM_ROW_CHUNK):
        rsl = pl.ds(r0, NORM_ROW_CHUNK)
        ssq = None
        for cb in range(nblk):
            cc = c_ref[cb, rsl, :]
            ssq = cc * cc if ssq is None else ssq + cc * cc
        r = jax.lax.rsqrt(jnp.sum(ssq, axis=-1, keepdims=True) * (1.0 / d) + EPS)
        r = jnp.broadcast_to(r, (NORM_ROW_CHUNK, LANES))
        for cb in range(nblk):
            cols = pl.ds(cb * LANES, LANES)
            y = (c_ref[cb, rsl, :] * r) * gn_ref[:, cols]
            y_ref[rsl, cols] = (y * jax.nn.sigmoid(y)).astype(BF16)
    o_ref[...] = h_ref[...] + _dot(y_ref[...], w2_ref[...]) + b2_ref[...]


def _mix_a_out(v3, h, w_dw3, b_dw3, g_norm, w2, b2, *, tm, seq):
    t, d = h.shape
    halo = CONV_A_HALO
    nblk = d // LANES
    kern = functools.partial(_mix_a_out_kernel, tm=tm, seq=seq)
    return pl.pallas_call(
        kern,
        grid=(t // tm,),
        in_specs=[
            pl.BlockSpec((nblk, tm, LANES), lambda i: (0, i, 0)),
            pl.BlockSpec((nblk, halo, LANES),
                         lambda i: (0, jnp.maximum(i * (tm // halo) - 1, 0), 0)),
            pl.BlockSpec((tm, d), lambda i: (i, 0)),
            pl.BlockSpec(w_dw3.shape, lambda i: (0, 0, 0)),
            pl.BlockSpec((nblk, 1, LANES), lambda i: (0, 0, 0)),
            pl.BlockSpec((1, d), lambda i: (0, 0)),
            pl.BlockSpec((d, d), lambda i: (0, 0)),
            pl.BlockSpec((1, d), lambda i: (0, 0)),
        ],
        out_specs=pl.BlockSpec((tm, d), lambda i: (i, 0)),
        out_shape=jax.ShapeDtypeStruct((t, d), F32),
        scratch_shapes=[pltpu.VMEM((nblk, tm + halo, LANES), F32),
                        pltpu.VMEM((nblk, tm, LANES), F32),
                        pltpu.VMEM((tm, d), BF16)],
        compiler_params=_params(1),
        name="mix_a_out",
    )(v3, v3, h, w_dw3, b_dw3, g_norm, w2, b2)


def _mix_b_in_kernel(x_ref, g_ref, wb_ref, wc_ref, wv_ref, gb_ref, cv_ref, u_ref):
    @pl.when(pl.program_id(1) == 0)
    def _():
        u_ref[...] = _rmsnorm(x_ref[...], g_ref[...]).astype(BF16)

    u = u_ref[...]
    gb_ref[...] = _dot(u, wb_ref[...])
    _store_lane_blocks(cv_ref, _dot(u, wc_ref[...]) * _dot(u, wv_ref[...]))


def _mix_b_in(h, g, w_in, *, tm, tn):
    t, d = h.shape
    nj = d // tn
    return pl.pallas_call(
        _mix_b_in_kernel,
        grid=(t // tm, nj),
        in_specs=[
            pl.BlockSpec((tm, d), lambda i, j: (i, 0)),
            pl.BlockSpec((1, d), lambda i, j: (0, 0)),
            pl.BlockSpec((d, tn), lambda i, j: (0, j)),
            pl.BlockSpec((d, tn), lambda i, j: (0, j + nj)),
            pl.BlockSpec((d, tn), lambda i, j: (0, j + 2 * nj)),
        ],
        out_specs=[pl.BlockSpec((tm, tn), lambda i, j: (i, j)),
                   pl.BlockSpec((tn // LANES, tm, LANES), lambda i, j: (j, i, 0))],
        out_shape=[jax.ShapeDtypeStruct((t, d), F32),
                   jax.ShapeDtypeStruct((d // LANES, t, LANES), F32)],
        scratch_shapes=[pltpu.VMEM((tm, d), BF16)],
        compiler_params=_params(2),
        name="mix_b_in",
    )(h, g, w_in, w_in, w_in)


def _mix_b_out_kernel(cv_ref, cvprev_ref, gb_ref, h_ref, wc_ref, wo_ref, o_ref, xe_ref, y_ref,
                      *, tm, seq):
    halo, width = CONV_B_HALO, CONV_B_WIDTH
    _fill_history(xe_ref, cv_ref, cvprev_ref, halo, tm, seq)
    for cb in range(cv_ref.shape[0]):
        cols = pl.ds(cb * LANES, LANES)
        conv = _conv_taps(xe_ref, wc_ref, cb, 0, tm, halo, width)
        y_ref[:, cols] = (gb_ref[:, cols] * conv).astype(BF16)
    o_ref[...] = h_ref[...] + _dot(y_ref[...], wo_ref[...])


def _mix_b_out(cv3, gb, h, w_conv3, w_out, *, tm, seq):
    t, d = h.shape
    halo = CONV_B_HALO
    nblk = d // LANES
    kern = functools.partial(_mix_b_out_kernel, tm=tm, seq=seq)
    return pl.pallas_call(
        kern,
        grid=(t // tm,),
        in_specs=[
            pl.BlockSpec((nblk, tm, LANES), lambda i: (0, i, 0)),
            pl.BlockSpec((nblk, halo, LANES),
                         lambda i: (0, jnp.maximum(i * (tm // halo) - 1, 0), 0)),
            pl.BlockSpec((tm, d), lambda i: (i, 0)),
            pl.BlockSpec((tm, d), lambda i: (i, 0)),
            pl.BlockSpec(w_conv3.shape, lambda i: (0, 0, 0)),
            pl.BlockSpec((d, d), lambda i: (0, 0)),
        ],
        out_specs=pl.BlockSpec((tm, d), lambda i: (i, 0)),
        out_shape=jax.ShapeDtypeStruct((t, d), F32),
        scratch_shapes=[pltpu.VMEM((nblk, tm + halo, LANES), F32), pltpu.VMEM((tm, d), BF16)],
        compiler_params=_params(1),
        name="mix_b_out",
    )(cv3, cv3, gb, h, w_conv3, w_out)


def _mlp_kernel(x_ref, g_ref, w1_ref, w2_ref, o_ref, u_ref, *, tn):
    @pl.when(pl.program_id(1) == 0)
    def _():
        x = x_ref[...]
        u_ref[...] = _rmsnorm(x, g_ref[...]).astype(BF16)
        o_ref[...] = x

    hid = jnp.maximum(_dot(u_ref[...], w1_ref[...]), 0.0)
    hid = (hid * hid).astype(BF16)
    for n in range(o_ref.shape[1] // tn):
        cols = pl.ds(n * tn, tn)
        o_ref[:, cols] += _dot(hid, w2_ref[:, cols])


def _mlp(h, g, w1, w2, *, tm, tf):
    t, d = h.shape
    f = w1.shape[1]
    kern = functools.partial(_mlp_kernel, tn=512)
    return pl.pallas_call(
        kern,
        grid=(t // tm, f // tf),
        in_specs=[
            pl.BlockSpec((tm, d), lambda i, j: (i, 0)),
            pl.BlockSpec((1, d), lambda i, j: (0, 0)),
            pl.BlockSpec((d, tf), lambda i, j: (0, j)),
            pl.BlockSpec((tf, d), lambda i, j: (j, 0)),
        ],
        out_specs=pl.BlockSpec((tm, d), lambda i, j: (i, 0)),
        out_shape=jax.ShapeDtypeStruct((t, d), F32),
        scratch_shapes=[pltpu.VMEM((tm, d), BF16)],
        compiler_params=_params(2),
        name="mlp",
    )(h, g, w1, w2)


def _ple_kernel(x_ref, p_ref, g_ref, wg_ref, wp_ref, o_ref):
    x = x_ref[...]
    u = _rmsnorm(x, g_ref[...]).astype(BF16)
    gate = jax.nn.sigmoid(_dot(u, wg_ref[...]))
    e = _dot(p_ref[...].astype(BF16), wp_ref[...])
    o_ref[...] = x + gate * e


def _ple(h, p, g, w_gate, w_proj, *, tm):
    t, d = h.shape
    k = p.shape[1]
    return pl.pallas_call(
        _ple_kernel,
        grid=(t // tm,),
        in_specs=[
            pl.BlockSpec((tm, d), lambda i: (i, 0)),
            pl.BlockSpec((tm, k), lambda i: (i, 0)),
            pl.BlockSpec((1, d), lambda i: (0, 0)),
            pl.BlockSpec((d, d), lambda i: (0, 0)),
            pl.BlockSpec((k, d), lambda i: (0, 0)),
        ],
        out_specs=pl.BlockSpec((tm, d), lambda i: (i, 0)),
        out_shape=jax.ShapeDtypeStruct((t, d), F32),
        compiler_params=_params(1),
        name="ple",
    )(h, p, g, w_gate, w_proj)


def _final_norm_kernel(x_ref, g_ref, o_ref):
    o_ref[...] = _rmsnorm(x_ref[...], g_ref[...])


def _final_norm(h, g, *, tm):
    t, d = h.shape
    return pl.pallas_call(
        _final_norm_kernel,
        grid=(t // tm,),
        in_specs=[pl.BlockSpec((tm, d), lambda i: (i, 0)),
                  pl.BlockSpec((1, d), lambda i: (0, 0))],
        out_specs=pl.BlockSpec((tm, d), lambda i: (i, 0)),
        out_shape=jax.ShapeDtypeStruct((t, d), F32),
        compiler_params=_params(1),
        name="final_norm",
    )(h, g)


def _lane_blocks(w, rows):
    k, d = w.shape
    w = jnp.pad(w, ((0, rows - k), (0, 0)))
    return w.reshape(rows, d // LANES, LANES).transpose(1, 0, 2)


def kernel(x, p, norm_mix, norm_mlp, norm_ple, cf_w_pw1, cf_b_pw1, cf_w_dw, cf_b_dw, cf_norm, cf_w_pw2, cf_b_pw2, sc_w_in, sc_w_conv, sc_w_out, mlp_w1, mlp_w2, ple_w_proj, ple_w_gate, norm_final):
    batch, seq, d = x.shape
    depth = p.shape[0]
    t = batch * seq
    h = x.reshape(t, d)
    p = p.reshape(depth, t, p.shape[-1])
    row = lambda a: a.reshape(1, -1)

    for i in range(depth):
        j = i // 2
        g_mix = row(norm_mix[i])
        if i % 2 == 0:
            v3 = _mix_a_in(h, g_mix, cf_w_pw1[j].astype(BF16), row(cf_b_pw1[j]), tm=1024, tn=512)
            h = _mix_a_out(v3, h, _lane_blocks(cf_w_dw[j], 32), _lane_blocks(row(cf_b_dw[j]), 1),
                           row(cf_norm[j]), cf_w_pw2[j].astype(BF16), row(cf_b_pw2[j]),
                           tm=512, seq=seq)
        else:
            gb, cv3 = _mix_b_in(h, g_mix, sc_w_in[j].astype(BF16), tm=1024, tn=512)
            h = _mix_b_out(cv3, gb, h, _lane_blocks(sc_w_conv[j], 8), sc_w_out[j].astype(BF16),
                           tm=512, seq=seq)
        h = _mlp(h, row(norm_mlp[i]), mlp_w1[i].astype(BF16), mlp_w2[i].astype(BF16),
                 tm=1024, tf=512)
        h = _ple(h, p[i], row(norm_ple[i]), ple_w_gate[i].astype(BF16),
                 ple_w_proj[i].astype(BF16), tm=512)

    return _final_norm(h, row(norm_final), tm=512).reshape(batch, seq, d)
```

```python
import functools

import jax
import jax.numpy as jnp
from jax.experimental import pallas as pl
from jax.experimental.pallas import tpu as pltpu

EPS = 1e-6
LANES = 128
CONV_A_WIDTH = 31
CONV_B_WIDTH = 3
CONV_A_HALO = 32
CONV_B_HALO = 8
CONV_ROW_CHUNK = 64
NORM_ROW_CHUNK = 128
V7X_VMEM_LIMIT_BYTES = 56 * 1024 * 1024

F32 = jnp.float32
BF16 = jnp.bfloat16


def _params(n_axes):
    sem = ("parallel",) + ("arbitrary",) * (n_axes - 1)
    return pltpu.CompilerParams(dimension_semantics=sem,
                                vmem_limit_bytes=V7X_VMEM_LIMIT_BYTES)


def _rmsnorm(x, g):
    r = jax.lax.rsqrt(jnp.mean(x * x, axis=-1, keepdims=True) + EPS)
    return (x * r) * g


def _dot(a, b):
    return jnp.dot(a, b, preferred_element_type=F32)


def _store_lane_blocks(o_ref, val):
    for c in range(o_ref.shape[0]):
        o_ref[c] = val[:, c * LANES:(c + 1) * LANES]


def _fill_history(xe_ref, cur_ref, prev_ref, halo, tm, seq):
    first = (pl.program_id(0) * tm) % seq == 0
    keep = jnp.where(first, 0.0, 1.0).astype(F32)
    xe_ref[:, pl.ds(0, halo), :] = prev_ref[...] * keep
    xe_ref[:, pl.ds(halo, tm), :] = cur_ref[...]


def _conv_taps(xe_ref, w_ref, cb, r0, rows, halo, width):
    base = halo - (width - 1)
    acc = None
    for k in range(width):
        term = xe_ref[cb, pl.ds(r0 + base + k, rows), :] * w_ref[cb, pl.ds(k, 1), :]
        acc = term if acc is None else acc + term
    return acc


def _mix_a_in_kernel(x_ref, g_ref, wv_ref, wg_ref, bv_ref, bg_ref, v_ref, u_ref):
    @pl.when(pl.program_id(1) == 0)
    def _():
        u_ref[...] = _rmsnorm(x_ref[...], g_ref[...]).astype(BF16)

    u = u_ref[...]
    val = _dot(u, wv_ref[...]) + bv_ref[...]
    gate = _dot(u, wg_ref[...]) + bg_ref[...]
    _store_lane_blocks(v_ref, val * jax.nn.sigmoid(gate))


def _mix_a_in(h, g, w1, b1, *, lg, lw, tm, tn):
    t, d = h.shape
    nj = d // tn
    return pl.pallas_call(
        _mix_a_in_kernel,
        grid=(t // tm, nj),
        in_specs=[
            pl.BlockSpec((tm, d), lambda i, j: (i, 0)),
            pl.BlockSpec((None, 1, d), lambda i, j: (lg, 0, 0)),
            pl.BlockSpec((None, d, tn), lambda i, j: (lw, 0, j)),
            pl.BlockSpec((None, d, tn), lambda i, j: (lw, 0, j + nj)),
            pl.BlockSpec((None, 1, tn), lambda i, j: (lw, 0, j)),
            pl.BlockSpec((None, 1, tn), lambda i, j: (lw, 0, j + nj)),
        ],
        out_specs=pl.BlockSpec((tn // LANES, tm, LANES), lambda i, j: (j, i, 0)),
        out_shape=jax.ShapeDtypeStruct((d // LANES, t, LANES), F32),
        scratch_shapes=[pltpu.VMEM((tm, d), BF16)],
        compiler_params=_params(2),
        name="mix_a_in",
    )(h, g, w1, w1, b1, b1)


def _mix_a_out_kernel(v_ref, vprev_ref, h_ref, wdw_ref, bdw_ref, gn_ref, w2_ref, b2_ref,
                      o_ref, xe_ref, c_ref, y_ref, *, tm, seq):
    halo, width, rows = CONV_A_HALO, CONV_A_WIDTH, CONV_ROW_CHUNK
    nblk = v_ref.shape[0]
    d = nblk * LANES
    _fill_history(xe_ref, v_ref, vprev_ref, halo, tm, seq)

    def lane_block(cb, carry):
        for r0 in range(0, tm, rows):
            c = _conv_taps(xe_ref, wdw_ref, cb, r0, rows, halo, width) + bdw_ref[cb]
            c_ref[cb, pl.ds(r0, rows), :] = c
        return carry

    jax.lax.fori_loop(0, nblk, lane_block, 0)

    for r0 in range(0, tm, NORM_ROW_CHUNK):
        rsl = pl.ds(r0, NORM_ROW_CHUNK)
        ssq = None
        for cb in range(nblk):
            cc = c_ref[cb, rsl, :]
            ssq = cc * cc if ssq is None else ssq + cc * cc
        r = jax.lax.rsqrt(jnp.sum(ssq, axis=-1, keepdims=True) * (1.0 / d) + EPS)
        r = jnp.broadcast_to(r, (NORM_ROW_CHUNK, LANES))
        for cb in range(nblk):
            cols = pl.ds(cb * LANES, LANES)
            y = (c_ref[cb, rsl, :] * r) * gn_ref[:, cols]
            y_ref[rsl, cols] = (y * jax.nn.sigmoid(y)).astype(BF16)
    o_ref[...] = h_ref[...] + _dot(y_ref[...], w2_ref[...]) + b2_ref[...]


def _mix_a_out(v3, h, w_dw3, b_dw3, g_norm, w2, b2, *, lw, tm, seq):
    t, d = h.shape
    halo = CONV_A_HALO
    nblk = d // LANES
    kern = functools.partial(_mix_a_out_kernel, tm=tm, seq=seq)
    return pl.pallas_call(
        kern,
        grid=(t // tm,),
        in_specs=[
            pl.BlockSpec((nblk, tm, LANES), lambda i: (0, i, 0)),
            pl.BlockSpec((nblk, halo, LANES),
                         lambda i: (0, jnp.maximum(i * (tm // halo) - 1, 0), 0)),
            pl.BlockSpec((tm, d), lambda i: (i, 0)),
            pl.BlockSpec((None,) + w_dw3.shape[1:], lambda i: (lw, 0, 0, 0)),
            pl.BlockSpec((None, nblk, 1, LANES), lambda i: (lw, 0, 0, 0)),
            pl.BlockSpec((None, 1, d), lambda i: (lw, 0, 0)),
            pl.BlockSpec((None, d, d), lambda i: (lw, 0, 0)),
            pl.BlockSpec((None, 1, d), lambda i: (lw, 0, 0)),
        ],
        out_specs=pl.BlockSpec((tm, d), lambda i: (i, 0)),
        out_shape=jax.ShapeDtypeStruct((t, d), F32),
        scratch_shapes=[pltpu.VMEM((nblk, tm + halo, LANES), F32),
                        pltpu.VMEM((nblk, tm, LANES), F32),
                        pltpu.VMEM((tm, d), BF16)],
        compiler_params=_params(1),
        name="mix_a_out",
    )(v3, v3, h, w_dw3, b_dw3, g_norm, w2, b2)


def _mix_b_in_kernel(x_ref, g_ref, wb_ref, wc_ref, wv_ref, gb_ref, cv_ref, u_ref):
    @pl.when(pl.program_id(1) == 0)
    def _():
        u_ref[...] = _rmsnorm(x_ref[...], g_ref[...]).astype(BF16)

    u = u_ref[...]
    gb_ref[...] = _dot(u, wb_ref[...])
    _store_lane_blocks(cv_ref, _dot(u, wc_ref[...]) * _dot(u, wv_ref[...]))


def _mix_b_in(h, g, w_in, *, lg, lw, tm, tn):
    t, d = h.shape
    nj = d // tn
    return pl.pallas_call(
        _mix_b_in_kernel,
        grid=(t // tm, nj),
        in_specs=[
            pl.BlockSpec((tm, d), lambda i, j: (i, 0)),
            pl.BlockSpec((None, 1, d), lambda i, j: (lg, 0, 0)),
            pl.BlockSpec((None, d, tn), lambda i, j: (lw, 0, j)),
            pl.BlockSpec((None, d, tn), lambda i, j: (lw, 0, j + nj)),
            pl.BlockSpec((None, d, tn), lambda i, j: (lw, 0, j + 2 * nj)),
        ],
        out_specs=[pl.BlockSpec((tm, tn), lambda i, j: (i, j)),
                   pl.BlockSpec((tn // LANES, tm, LANES), lambda i, j: (j, i, 0))],
        out_shape=[jax.ShapeDtypeStruct((t, d), F32),
                   jax.ShapeDtypeStruct((d // LANES, t, LANES), F32)],
        scratch_shapes=[pltpu.VMEM((tm, d), BF16)],
        compiler_params=_params(2),
        name="mix_b_in",
    )(h, g, w_in, w_in, w_in)


def _mix_b_out_kernel(cv_ref, cvprev_ref, gb_ref, h_ref, wc_ref, wo_ref, o_ref, xe_ref, y_ref,
                      *, tm, seq):
    halo, width = CONV_B_HALO, CONV_B_WIDTH
    _fill_history(xe_ref, cv_ref, cvprev_ref, halo, tm, seq)
    for cb in range(cv_ref.shape[0]):
        cols = pl.ds(cb * LANES, LANES)
        conv = _conv_taps(xe_ref, wc_ref, cb, 0, tm, halo, width)
        y_ref[:, cols] = (gb_ref[:, cols] * conv).astype(BF16)
    o_ref[...] = h_ref[...] + _dot(y_ref[...], wo_ref[...])


def _mix_b_out(cv3, gb, h, w_conv3, w_out, *, lw, tm, seq):
    t, d = h.shape
    halo = CONV_B_HALO
    nblk = d // LANES
    kern = functools.partial(_mix_b_out_kernel, tm=tm, seq=seq)
    return pl.pallas_call(
        kern,
        grid=(t // tm,),
        in_specs=[
            pl.BlockSpec((nblk, tm, LANES), lambda i: (0, i, 0)),
            pl.BlockSpec((nblk, halo, LANES),
                         lambda i: (0, jnp.maximum(i * (tm // halo) - 1, 0), 0)),
            pl.BlockSpec((tm, d), lambda i: (i, 0)),
            pl.BlockSpec((tm, d), lambda i: (i, 0)),
            pl.BlockSpec((None,) + w_conv3.shape[1:], lambda i: (lw, 0, 0, 0)),
            pl.BlockSpec((None, d, d), lambda i: (lw, 0, 0)),
        ],
        out_specs=pl.BlockSpec((tm, d), lambda i: (i, 0)),
        out_shape=jax.ShapeDtypeStruct((t, d), F32),
        scratch_shapes=[pltpu.VMEM((nblk, tm + halo, LANES), F32), pltpu.VMEM((tm, d), BF16)],
        compiler_params=_params(1),
        name="mix_b_out",
    )(cv3, cv3, gb, h, w_conv3, w_out)


def _mlp_kernel(x_ref, g_ref, w1_ref, w2_ref, o_ref, u_ref, *, tn):
    @pl.when(pl.program_id(1) == 0)
    def _():
        x = x_ref[...]
        u_ref[...] = _rmsnorm(x, g_ref[...]).astype(BF16)
        o_ref[...] = x

    hid = jnp.maximum(_dot(u_ref[...], w1_ref[...]), 0.0)
    hid = (hid * hid).astype(BF16)
    for n in range(o_ref.shape[1] // tn):
        cols = pl.ds(n * tn, tn)
        o_ref[:, cols] += _dot(hid, w2_ref[:, cols])


def _mlp(h, g, w1, w2, *, lg, tm, tf):
    t, d = h.shape
    f = w1.shape[2]
    kern = functools.partial(_mlp_kernel, tn=512)
    return pl.pallas_call(
        kern,
        grid=(t // tm, f // tf),
        in_specs=[
            pl.BlockSpec((tm, d), lambda i, j: (i, 0)),
            pl.BlockSpec((None, 1, d), lambda i, j: (lg, 0, 0)),
            pl.BlockSpec((None, d, tf), lambda i, j: (lg, 0, j)),
            pl.BlockSpec((None, tf, d), lambda i, j: (lg, j, 0)),
        ],
        out_specs=pl.BlockSpec((tm, d), lambda i, j: (i, 0)),
        out_shape=jax.ShapeDtypeStruct((t, d), F32),
        scratch_shapes=[pltpu.VMEM((tm, d), BF16)],
        compiler_params=_params(2),
        name="mlp",
    )(h, g, w1, w2)


def _ple_kernel(x_ref, p_ref, g_ref, wg_ref, wp_ref, gf_ref, o_ref, *, close_trunk):
    x = x_ref[...]
    u = _rmsnorm(x, g_ref[...]).astype(BF16)
    gate = jax.nn.sigmoid(_dot(u, wg_ref[...]))
    e = _dot(p_ref[...].astype(BF16), wp_ref[...])
    out = x + gate * e
    o_ref[...] = _rmsnorm(out, gf_ref[...]) if close_trunk else out


def _ple(h, p, g, w_gate, w_proj, g_final, *, lg, tm, close_trunk):
    t, d = h.shape
    k = p.shape[2]
    kern = functools.partial(_ple_kernel, close_trunk=close_trunk)
    return pl.pallas_call(
        kern,
        grid=(t // tm,),
        in_specs=[
            pl.BlockSpec((tm, d), lambda i: (i, 0)),
            pl.BlockSpec((None, tm, k), lambda i: (lg, i, 0)),
            pl.BlockSpec((None, 1, d), lambda i: (lg, 0, 0)),
            pl.BlockSpec((None, d, d), lambda i: (lg, 0, 0)),
            pl.BlockSpec((None, k, d), lambda i: (lg, 0, 0)),
            pl.BlockSpec((1, d), lambda i: (0, 0)),
        ],
        out_specs=pl.BlockSpec((tm, d), lambda i: (i, 0)),
        out_shape=jax.ShapeDtypeStruct((t, d), F32),
        compiler_params=_params(1),
        name="ple",
    )(h, p, g, w_gate, w_proj, g_final)


def _lane_blocks(w, rows):
    n, k, d = w.shape
    w = jnp.pad(w, ((0, 0), (0, rows - k), (0, 0)))
    return w.reshape(n, rows, d // LANES, LANES).transpose(0, 2, 1, 3)


def kernel(x, p, norm_mix, norm_mlp, norm_ple, cf_w_pw1, cf_b_pw1, cf_w_dw, cf_b_dw, cf_norm, cf_w_pw2, cf_b_pw2, sc_w_in, sc_w_conv, sc_w_out, mlp_w1, mlp_w2, ple_w_proj, ple_w_gate, norm_final):
    batch, seq, d = x.shape
    depth = p.shape[0]
    t = batch * seq
    h = x.reshape(t, d)
    p = p.reshape(depth, t, p.shape[-1])
    rows = lambda a: a[:, None, :]
    bf = lambda a: a.astype(BF16)

    norm_mix, norm_mlp, norm_ple = rows(norm_mix), rows(norm_mlp), rows(norm_ple)
    cf_w_pw1, cf_w_pw2, sc_w_in, sc_w_out = bf(cf_w_pw1), bf(cf_w_pw2), bf(sc_w_in), bf(sc_w_out)
    mlp_w1, mlp_w2, ple_w_gate, ple_w_proj = bf(mlp_w1), bf(mlp_w2), bf(ple_w_gate), bf(ple_w_proj)
    cf_b_pw1, cf_norm, cf_b_pw2 = rows(cf_b_pw1), rows(cf_norm), rows(cf_b_pw2)
    cf_w_dw3 = _lane_blocks(cf_w_dw, CONV_A_HALO)
    cf_b_dw3 = _lane_blocks(rows(cf_b_dw), 1)
    sc_w_conv3 = _lane_blocks(sc_w_conv, CONV_B_HALO)
    g_final = norm_final.reshape(1, d)

    for i in range(depth):
        j = i // 2
        if i % 2 == 0:
            v3 = _mix_a_in(h, norm_mix, cf_w_pw1, cf_b_pw1, lg=i, lw=j, tm=1024, tn=512)
            h = _mix_a_out(v3, h, cf_w_dw3, cf_b_dw3, cf_norm, cf_w_pw2, cf_b_pw2,
                           lw=j, tm=512, seq=seq)
        else:
            gb, cv3 = _mix_b_in(h, norm_mix, sc_w_in, lg=i, lw=j, tm=1024, tn=512)
            h = _mix_b_out(cv3, gb, h, sc_w_conv3, sc_w_out, lw=j, tm=512, seq=seq)
        h = _mlp(h, norm_mlp, mlp_w1, mlp_w2, lg=i, tm=1024, tf=512)
        h = _ple(h, p, norm_ple, ple_w_gate, ple_w_proj, g_final, lg=i, tm=512,
                 close_trunk=(i == depth - 1))

    return h.reshape(batch, seq, d)
```

```python
import functools

import jax
import jax.numpy as jnp
from jax.experimental import pallas as pl
from jax.experimental.pallas import tpu as pltpu

EPS = 1e-6
LANES = 128
CONV_A_WIDTH = 31
CONV_B_WIDTH = 3
CONV_A_HALO = 32
CONV_B_HALO = 8
CONV_ROW_CHUNK = 64
NORM_ROW_CHUNK = 128
SHORT_CONV_ROW_CHUNK = 256
PROJ_COLS = 256
V7X_VMEM_LIMIT_BYTES = 56 * 1024 * 1024

F32 = jnp.float32
BF16 = jnp.bfloat16


def _params(n_axes, ordered_rows=False):
    sem = ("arbitrary" if ordered_rows else "parallel",) + ("arbitrary",) * (n_axes - 1)
    return pltpu.CompilerParams(dimension_semantics=sem,
                                vmem_limit_bytes=V7X_VMEM_LIMIT_BYTES)


def _rmsnorm(x, g):
    r = jax.lax.rsqrt(jnp.mean(x * x, axis=-1, keepdims=True) + EPS)
    return (x * r) * g


def _dot(a, b):
    return jnp.dot(a, b.astype(BF16), preferred_element_type=F32)


def _store_lane_blocks(o_ref, val):
    for c in range(o_ref.shape[0]):
        o_ref[c] = val[:, c * LANES:(c + 1) * LANES]


def _fill_history(xe_ref, cur_ref, prev_ref, halo, tm, seq):
    first = (pl.program_id(0) * tm) % seq == 0
    keep = jnp.where(first, 0.0, 1.0).astype(F32)
    xe_ref[:, pl.ds(0, halo), :] = prev_ref[...] * keep
    xe_ref[:, pl.ds(halo, tm), :] = cur_ref[...]


def _conv_taps(xe_ref, w_ref, cb, r0, rows, halo, width):
    base = halo - (width - 1)
    acc = None
    for k in range(width):
        term = xe_ref[cb, pl.ds(r0 + base + k, rows), :] * w_ref[cb, pl.ds(k, 1), :]
        acc = term if acc is None else acc + term
    return acc


def _mix_a_in_kernel(x_ref, g_ref, wv_ref, wg_ref, bv_ref, bg_ref, v_ref, u_ref):
    @pl.when(pl.program_id(1) == 0)
    def _():
        u_ref[...] = _rmsnorm(x_ref[...], g_ref[...]).astype(BF16)

    u = u_ref[...]
    nb = PROJ_COLS // LANES
    for s in range(v_ref.shape[0] // nb):
        cols = pl.ds(s * PROJ_COLS, PROJ_COLS)
        val = _dot(u, wv_ref[:, cols]) + bv_ref[:, cols]
        gate = _dot(u, wg_ref[:, cols]) + bg_ref[:, cols]
        _store_lane_blocks(v_ref.at[pl.ds(s * nb, nb)], val * jax.nn.sigmoid(gate))


def _mix_a_in(h, g, w1, b1, *, lg, lw, tm, tn):
    t, d = h.shape
    nj = d // tn
    return pl.pallas_call(
        _mix_a_in_kernel,
        grid=(t // tm, nj),
        in_specs=[
            pl.BlockSpec((tm, d), lambda i, j: (i, 0)),
            pl.BlockSpec((None, 1, d), lambda i, j: (lg, 0, 0)),
            pl.BlockSpec((None, d, tn), lambda i, j: (lw, 0, j)),
            pl.BlockSpec((None, d, tn), lambda i, j: (lw, 0, j + nj)),
            pl.BlockSpec((None, 1, tn), lambda i, j: (lw, 0, j)),
            pl.BlockSpec((None, 1, tn), lambda i, j: (lw, 0, j + nj)),
        ],
        out_specs=pl.BlockSpec((tn // LANES, tm, LANES), lambda i, j: (j, i, 0)),
        out_shape=jax.ShapeDtypeStruct((d // LANES, t, LANES), F32),
        scratch_shapes=[pltpu.VMEM((tm, d), BF16)],
        compiler_params=_params(2),
        name="mix_a_in",
    )(h, g, w1, w1, b1, b1)


def _mix_a_out_kernel(v_ref, vprev_ref, h_ref, wdw_ref, bdw_ref, gn_ref, w2_ref, b2_ref,
                      o_ref, xe_ref, c_ref, y_ref, *, tm, seq):
    halo, width, rows = CONV_A_HALO, CONV_A_WIDTH, CONV_ROW_CHUNK
    nblk = v_ref.shape[0]
    d = nblk * LANES
    _fill_history(xe_ref, v_ref, vprev_ref, halo, tm, seq)

    def lane_block(cb, carry):
        for r0 in range(0, tm, rows):
            c = _conv_taps(xe_ref, wdw_ref, cb, r0, rows, halo, width) + bdw_ref[cb]
            c_ref[cb, pl.ds(r0, rows), :] = c
        return carry

    jax.lax.fori_loop(0, nblk, lane_block, 0)

    for r0 in range(0, tm, NORM_ROW_CHUNK):
        rsl = pl.ds(r0, NORM_ROW_CHUNK)
        ssq = None
        for cb in range(nblk):
            cc = c_ref[cb, rsl, :]
            ssq = cc * cc if ssq is None else ssq + cc * cc
        r = jax.lax.rsqrt(jnp.sum(ssq, axis=-1, keepdims=True) * (1.0 / d) + EPS)
        r = jnp.broadcast_to(r, (NORM_ROW_CHUNK, LANES))
        for cb in range(nblk):
            cols = pl.ds(cb * LANES, LANES)
            y = (c_ref[cb, rsl, :] * r) * gn_ref[:, cols]
            y_ref[rsl, cols] = (y * jax.nn.sigmoid(y)).astype(BF16)
    o_ref[...] = h_ref[...] + _dot(y_ref[...], w2_ref[...]) + b2_ref[...]


def _mix_a_out(v3, h, w_dw3, b_dw3, g_norm, w2, b2, *, lw, tm, seq):
    t, d = h.shape
    halo = CONV_A_HALO
    nblk = d // LANES
    kern = functools.partial(_mix_a_out_kernel, tm=tm, seq=seq)
    return pl.pallas_call(
        kern,
        grid=(t // tm,),
        in_specs=[
            pl.BlockSpec((nblk, tm, LANES), lambda i: (0, i, 0)),
            pl.BlockSpec((nblk, halo, LANES),
                         lambda i: (0, jnp.maximum(i * (tm // halo) - 1, 0), 0)),
            pl.BlockSpec((tm, d), lambda i: (i, 0)),
            pl.BlockSpec((None,) + w_dw3.shape[1:], lambda i: (lw, 0, 0, 0)),
            pl.BlockSpec((None, nblk, 1, LANES), lambda i: (lw, 0, 0, 0)),
            pl.BlockSpec((None, 1, d), lambda i: (lw, 0, 0)),
            pl.BlockSpec((None, d, d), lambda i: (lw, 0, 0), pipeline_mode=pl.Buffered(1)),
            pl.BlockSpec((None, 1, d), lambda i: (lw, 0, 0)),
        ],
        out_specs=pl.BlockSpec((tm, d), lambda i: (i, 0)),
        out_shape=jax.ShapeDtypeStruct((t, d), F32),
        scratch_shapes=[pltpu.VMEM((nblk, tm + halo, LANES), F32),
                        pltpu.VMEM((nblk, tm, LANES), F32),
                        pltpu.VMEM((tm, d), BF16)],
        compiler_params=_params(1),
        name="mix_a_out",
    )(v3, v3, h, w_dw3, b_dw3, g_norm, w2, b2)


def _mix_b_in_kernel(x_ref, g_ref, wb_ref, wc_ref, wv_ref, wconv_ref, y_ref,
                     u_ref, xe_ref, hist_ref, *, tm, seq):
    halo, width, rows = CONV_B_HALO, CONV_B_WIDTH, SHORT_CONV_ROW_CHUNK
    i, j = pl.program_id(0), pl.program_id(1)
    nb = xe_ref.shape[0]

    @pl.when(j == 0)
    def _():
        u_ref[...] = _rmsnorm(x_ref[...], g_ref[...]).astype(BF16)

    @pl.when(i == 0)
    def _():
        for c in range(nb):
            hist_ref[j * nb + c] = jnp.zeros((halo, LANES), F32)

    u = u_ref[...]
    cv = _dot(u, wc_ref[...]) * _dot(u, wv_ref[...])
    seq_start = (i * tm) % seq == 0
    for c in range(nb):
        blk = j * nb + c
        cols = slice(c * LANES, (c + 1) * LANES)
        xe_ref[c, pl.ds(0, halo), :] = jnp.where(seq_start, 0.0, hist_ref[blk])
        xe_ref[c, pl.ds(halo, tm), :] = cv[:, cols]
        hist_ref[blk] = cv[tm - halo:, cols]
    gb = _dot(u, wb_ref[...])
    for c in range(nb):
        for r0 in range(0, tm, rows):
            conv = _conv_taps(xe_ref, wconv_ref, c, r0, rows, halo, width)
            gate = gb[r0:r0 + rows, c * LANES:(c + 1) * LANES]
            y_ref[pl.ds(r0, rows), pl.ds(c * LANES, LANES)] = (gate * conv).astype(BF16)


def _mix_b_in(h, g, w_in, w_conv3, *, lg, lw, tm, tn, seq):
    t, d = h.shape
    nj = d // tn
    nb = tn // LANES
    halo = CONV_B_HALO
    kern = functools.partial(_mix_b_in_kernel, tm=tm, seq=seq)
    return pl.pallas_call(
        kern,
        grid=(t // tm, nj),
        in_specs=[
            pl.BlockSpec((tm, d), lambda i, j: (i, 0)),
            pl.BlockSpec((None, 1, d), lambda i, j: (lg, 0, 0)),
            pl.BlockSpec((None, d, tn), lambda i, j: (lw, 0, j)),
            pl.BlockSpec((None, d, tn), lambda i, j: (lw, 0, j + nj)),
            pl.BlockSpec((None, d, tn), lambda i, j: (lw, 0, j + 2 * nj)),
            pl.BlockSpec((None, nb) + w_conv3.shape[2:], lambda i, j: (lw, j, 0, 0)),
        ],
        out_specs=pl.BlockSpec((tm, tn), lambda i, j: (i, j)),
        out_shape=jax.ShapeDtypeStruct((t, d), BF16),
        scratch_shapes=[pltpu.VMEM((tm, d), BF16),
                        pltpu.VMEM((nb, tm + halo, LANES), F32),
                        pltpu.VMEM((d // LANES, halo, LANES), F32)],
        compiler_params=_params(2, ordered_rows=True),
        name="mix_b_in",
    )(h, g, w_in, w_in, w_in, w_conv3)


def _mix_b_out_kernel(y_ref, h_ref, wo_ref, o_ref):
    o_ref[...] = h_ref[...] + _dot(y_ref[...], wo_ref[...])


def _mix_b_out(y, h, w_out, *, lw, tm):
    t, d = h.shape
    return pl.pallas_call(
        _mix_b_out_kernel,
        grid=(t // tm,),
        in_specs=[
            pl.BlockSpec((tm, d), lambda i: (i, 0)),
            pl.BlockSpec((tm, d), lambda i: (i, 0)),
            pl.BlockSpec((None, d, d), lambda i: (lw, 0, 0), pipeline_mode=pl.Buffered(1)),
        ],
        out_specs=pl.BlockSpec((tm, d), lambda i: (i, 0)),
        out_shape=jax.ShapeDtypeStruct((t, d), F32),
        compiler_params=_params(1),
        name="mix_b_out",
    )(y, h, w_out)


def _mlp_kernel(x_ref, g_ref, w1_ref, w2_ref, o_ref, u_ref, *, tn):
    @pl.when(pl.program_id(1) == 0)
    def _():
        x = x_ref[...]
        u_ref[...] = _rmsnorm(x, g_ref[...]).astype(BF16)
        o_ref[...] = x

    hid = jnp.maximum(_dot(u_ref[...], w1_ref[...]), 0.0)
    hid = (hid * hid).astype(BF16)
    for n in range(o_ref.shape[1] // tn):
        cols = pl.ds(n * tn, tn)
        o_ref[:, cols] += _dot(hid, w2_ref[:, cols])


def _mlp(h, g, w1, w2, *, lg, tm, tf):
    t, d = h.shape
    f = w1.shape[2]
    kern = functools.partial(_mlp_kernel, tn=512)
    return pl.pallas_call(
        kern,
        grid=(t // tm, f // tf),
        in_specs=[
            pl.BlockSpec((tm, d), lambda i, j: (i, 0)),
            pl.BlockSpec((None, 1, d), lambda i, j: (lg, 0, 0)),
            pl.BlockSpec((None, d, tf), lambda i, j: (lg, 0, j)),
            pl.BlockSpec((None, tf, d), lambda i, j: (lg, j, 0)),
        ],
        out_specs=pl.BlockSpec((tm, d), lambda i, j: (i, 0)),
        out_shape=jax.ShapeDtypeStruct((t, d), F32),
        scratch_shapes=[pltpu.VMEM((tm, d), BF16)],
        compiler_params=_params(2),
        name="mlp",
    )(h, g, w1, w2)


def _ple_kernel(x_ref, p_ref, g_ref, wg_ref, wp_ref, gf_ref, o_ref, *, close_trunk):
    x = x_ref[...]
    u = _rmsnorm(x, g_ref[...]).astype(BF16)
    gate = jax.nn.sigmoid(_dot(u, wg_ref[...]))
    e = _dot(p_ref[...].astype(BF16), wp_ref[...])
    out = x + gate * e
    o_ref[...] = _rmsnorm(out, gf_ref[...]) if close_trunk else out


def _ple(h, p, g, w_gate, w_proj, g_final, *, lg, tm, close_trunk):
    t, d = h.shape
    k = p.shape[2]
    kern = functools.partial(_ple_kernel, close_trunk=close_trunk)
    return pl.pallas_call(
        kern,
        grid=(t // tm,),
        in_specs=[
            pl.BlockSpec((tm, d), lambda i: (i, 0)),
            pl.BlockSpec((None, tm, k), lambda i: (lg, i, 0)),
            pl.BlockSpec((None, 1, d), lambda i: (lg, 0, 0)),
            pl.BlockSpec((None, d, d), lambda i: (lg, 0, 0), pipeline_mode=pl.Buffered(1)),
            pl.BlockSpec((None, k, d), lambda i: (lg, 0, 0), pipeline_mode=pl.Buffered(1)),
            pl.BlockSpec((1, d), lambda i: (0, 0)),
        ],
        out_specs=pl.BlockSpec((tm, d), lambda i: (i, 0)),
        out_shape=jax.ShapeDtypeStruct((t, d), F32),
        compiler_params=_params(1),
        name="ple",
    )(h, p, g, w_gate, w_proj, g_final)


def _lane_blocks(w, rows):
    n, k, d = w.shape
    w = jnp.pad(w, ((0, 0), (0, rows - k), (0, 0)))
    return w.reshape(n, rows, d // LANES, LANES).transpose(0, 2, 1, 3)


def kernel(x, p, norm_mix, norm_mlp, norm_ple, cf_w_pw1, cf_b_pw1, cf_w_dw, cf_b_dw, cf_norm, cf_w_pw2, cf_b_pw2, sc_w_in, sc_w_conv, sc_w_out, mlp_w1, mlp_w2, ple_w_proj, ple_w_gate, norm_final):
    batch, seq, d = x.shape
    depth = p.shape[0]
    t = batch * seq
    h = x.reshape(t, d)
    p = p.reshape(depth, t, p.shape[-1])
    rows = lambda a: a[:, None, :]

    norm_mix, norm_mlp, norm_ple = rows(norm_mix), rows(norm_mlp), rows(norm_ple)
    cf_b_pw1, cf_norm, cf_b_pw2 = rows(cf_b_pw1), rows(cf_norm), rows(cf_b_pw2)
    cf_w_pw2 = cf_w_pw2.astype(BF16)
    cf_w_dw3 = _lane_blocks(cf_w_dw, CONV_A_HALO)
    cf_b_dw3 = _lane_blocks(rows(cf_b_dw), 1)
    sc_w_conv3 = _lane_blocks(sc_w_conv, CONV_B_HALO)
    g_final = norm_final.reshape(1, d)

    for i in range(depth):
        j = i // 2
        if i % 2 == 0:
            v3 = _mix_a_in(h, norm_mix, cf_w_pw1, cf_b_pw1, lg=i, lw=j, tm=1024, tn=512)
            h = _mix_a_out(v3, h, cf_w_dw3, cf_b_dw3, cf_norm, cf_w_pw2, cf_b_pw2,
                           lw=j, tm=512, seq=seq)
        else:
            y = _mix_b_in(h, norm_mix, sc_w_in, sc_w_conv3, lg=i, lw=j, tm=1024, tn=512, seq=seq)
            h = _mix_b_out(y, h, sc_w_out, lw=j, tm=512)
        h = _mlp(h, norm_mlp, mlp_w1, mlp_w2, lg=i, tm=1024, tf=512)
        h = _ple(h, p, norm_ple, ple_w_gate, ple_w_proj, g_final, lg=i, tm=512,
                 close_trunk=(i == depth - 1))

    return h.reshape(batch, seq, d)
```

```python
import functools

import jax
import jax.numpy as jnp
from jax.experimental import pallas as pl
from jax.experimental.pallas import tpu as pltpu

EPS = 1e-6
LANES = 128
PACK = 16
CONV_A_WIDTH = 31
CONV_B_WIDTH = 3
CONV_A_HALO = 32
CONV_B_HALO = 8
CONV_ROW_CHUNK = 64
NORM_ROW_CHUNK = 128
SHORT_CONV_ROW_CHUNK = 256
PROJ_COLS = 256
V7X_VMEM_LIMIT_BYTES = 56 * 1024 * 1024

F32 = jnp.float32
BF16 = jnp.bfloat16


def _params(n_axes, ordered_rows=False):
    sem = ("arbitrary" if ordered_rows else "parallel",) + ("arbitrary",) * (n_axes - 1)
    return pltpu.CompilerParams(dimension_semantics=sem,
                                vmem_limit_bytes=V7X_VMEM_LIMIT_BYTES)


def _rmsnorm(x, g):
    r = jax.lax.rsqrt(jnp.mean(x * x, axis=-1, keepdims=True) + EPS)
    return (x * r) * g


def _dot(a, b):
    return jnp.dot(a, b.astype(BF16), preferred_element_type=F32)


def _store_lane_blocks(o_ref, val):
    for c in range(o_ref.shape[0]):
        o_ref[c] = val[:, c * LANES:(c + 1) * LANES]


def _conv_taps(xe_ref, w_ref, cb, r0, rows, halo, width):
    base = halo - (width - 1)
    acc = None
    for k in range(width):
        term = xe_ref[cb, pl.ds(r0 + base + k, rows), :] * w_ref[cb, pl.ds(k, 1), :]
        acc = term if acc is None else acc + term
    return acc


def _mix_a_in_kernel(x_ref, g_ref, wv_ref, wg_ref, bv_ref, bg_ref, v_ref, u_ref):
    @pl.when(pl.program_id(1) == 0)
    def _():
        u_ref[...] = _rmsnorm(x_ref[...], g_ref[...]).astype(BF16)

    u = u_ref[...]
    nb = PROJ_COLS // LANES
    for s in range(v_ref.shape[0] // nb):
        cols = pl.ds(s * PROJ_COLS, PROJ_COLS)
        val = _dot(u, wv_ref[:, cols]) + bv_ref[:, cols]
        gate = _dot(u, wg_ref[:, cols]) + bg_ref[:, cols]
        v = (val * jax.nn.sigmoid(gate)).astype(BF16)
        _store_lane_blocks(v_ref.at[pl.ds(s * nb, nb)], v)


def _mix_a_in(h, g, w1, b1, *, lg, lw, tm, tn):
    t, d = h.shape
    nj = d // tn
    return pl.pallas_call(
        _mix_a_in_kernel,
        grid=(t // tm, nj),
        in_specs=[
            pl.BlockSpec((tm, d), lambda i, j: (i, 0)),
            pl.BlockSpec((None, 1, d), lambda i, j: (lg, 0, 0)),
            pl.BlockSpec((None, d, tn), lambda i, j: (lw, 0, j)),
            pl.BlockSpec((None, d, tn), lambda i, j: (lw, 0, j + nj)),
            pl.BlockSpec((None, 1, tn), lambda i, j: (lw, 0, j)),
            pl.BlockSpec((None, 1, tn), lambda i, j: (lw, 0, j + nj)),
        ],
        out_specs=pl.BlockSpec((tn // LANES, tm, LANES), lambda i, j: (j, i, 0)),
        out_shape=jax.ShapeDtypeStruct((d // LANES, t, LANES), BF16),
        scratch_shapes=[pltpu.VMEM((tm, d), BF16)],
        compiler_params=_params(2),
        name="mix_a_in",
    )(h, g, w1, w1, b1, b1)


def _mix_a_out_kernel(v_ref, vprev_ref, h_ref, wrep_ref, bdw_ref, gn_ref, w2_ref, b2_ref,
                      o_ref, xe_ref, xo_ref, xf_ref, c_ref, y_ref, *, tm, seq):
    halo, width, rows = CONV_A_HALO, CONV_A_WIDTH, CONV_ROW_CHUNK
    nblk = v_ref.shape[0]
    d = nblk * LANES
    span = halo + tm
    first = (pl.program_id(0) * tm) % seq == 0
    xe_ref[:, pl.ds(0, halo), :] = jnp.where(first, jnp.zeros_like(vprev_ref), vprev_ref[...])
    xe_ref[:, pl.ds(halo, tm), :] = v_ref[...]
    xe_ref[:, pl.ds(span, PACK), :] = jnp.zeros((nblk, PACK, LANES), BF16)
    even_words = xe_ref.bitcast(jnp.uint32)
    odd_words = xo_ref.bitcast(jnp.uint32)
    base = halo - (width - 1)

    def lane_block(cb, carry):
        xf_ref[...] = xe_ref[cb].astype(F32)
        xo_ref[cb, pl.ds(0, span), :] = xf_ref[pl.ds(1, span), :].astype(BF16)
        for r0 in range(0, tm, rows):
            acc = [None] * (rows // PACK)
            for k in range(width):
                first_row = r0 + base + k
                words = even_words if first_row % 2 == 0 else odd_words
                w = wrep_ref[cb, k].astype(F32)
                for q in range(rows // PACK):
                    win = words[cb, pl.ds(first_row // 2 + q * (PACK // 2), PACK // 2), :]
                    term = pltpu.bitcast(win, BF16).astype(F32) * w
                    acc[q] = term if acc[q] is None else acc[q] + term
            for q in range(rows // PACK):
                c_ref[cb, pl.ds(r0 + q * PACK, PACK), :] = acc[q] + bdw_ref[cb]
        return carry

    jax.lax.fori_loop(0, nblk, lane_block, 0)

    for r0 in range(0, tm, NORM_ROW_CHUNK):
        rsl = pl.ds(r0, NORM_ROW_CHUNK)
        ssq = None
        for cb in range(nblk):
            cc = c_ref[cb, rsl, :]
            ssq = cc * cc if ssq is None else ssq + cc * cc
        r = jax.lax.rsqrt(jnp.sum(ssq, axis=-1, keepdims=True) * (1.0 / d) + EPS)
        r = jnp.broadcast_to(r, (NORM_ROW_CHUNK, LANES))
        for cb in range(nblk):
            cols = pl.ds(cb * LANES, LANES)
            y = (c_ref[cb, rsl, :] * r) * gn_ref[:, cols]
            y_ref[rsl, cols] = (y * jax.nn.sigmoid(y)).astype(BF16)
    o_ref[...] = h_ref[...] + _dot(y_ref[...], w2_ref[...]) + b2_ref[...]


def _mix_a_out(v3, h, w_rep, b_dw3, g_norm, w2, b2, *, lw, tm, seq):
    t, d = h.shape
    halo = CONV_A_HALO
    nblk = d // LANES
    kern = functools.partial(_mix_a_out_kernel, tm=tm, seq=seq)
    return pl.pallas_call(
        kern,
        grid=(t // tm,),
        in_specs=[
            pl.BlockSpec((nblk, tm, LANES), lambda i: (0, i, 0)),
            pl.BlockSpec((nblk, halo, LANES),
                         lambda i: (0, jnp.maximum(i * (tm // halo) - 1, 0), 0)),
            pl.BlockSpec((tm, d), lambda i: (i, 0)),
            pl.BlockSpec((None,) + w_rep.shape[1:], lambda i: (lw, 0, 0, 0, 0)),
            pl.BlockSpec((None, nblk, 1, LANES), lambda i: (lw, 0, 0, 0)),
            pl.BlockSpec((None, 1, d), lambda i: (lw, 0, 0)),
            pl.BlockSpec((None, d, d), lambda i: (lw, 0, 0), pipeline_mode=pl.Buffered(1)),
            pl.BlockSpec((None, 1, d), lambda i: (lw, 0, 0)),
        ],
        out_specs=pl.BlockSpec((tm, d), lambda i: (i, 0)),
        out_shape=jax.ShapeDtypeStruct((t, d), F32),
        scratch_shapes=[pltpu.VMEM((nblk, halo + tm + PACK, LANES), BF16),
                        pltpu.VMEM((nblk, halo + tm + PACK, LANES), BF16),
                        pltpu.VMEM((halo + tm + PACK, LANES), F32),
                        pltpu.VMEM((nblk, tm, LANES), F32),
                        pltpu.VMEM((tm, d), BF16)],
        compiler_params=_params(1),
        name="mix_a_out",
    )(v3, v3, h, w_rep, b_dw3, g_norm, w2, b2)


def _mix_b_in_kernel(x_ref, g_ref, wb_ref, wc_ref, wv_ref, wconv_ref, y_ref,
                     u_ref, xe_ref, hist_ref, *, tm, seq):
    halo, width, rows = CONV_B_HALO, CONV_B_WIDTH, SHORT_CONV_ROW_CHUNK
    i, j = pl.program_id(0), pl.program_id(1)
    nb = xe_ref.shape[0]

    @pl.when(j == 0)
    def _():
        u_ref[...] = _rmsnorm(x_ref[...], g_ref[...]).astype(BF16)

    @pl.when(i == 0)
    def _():
        for c in range(nb):
            hist_ref[j * nb + c] = jnp.zeros((halo, LANES), F32)

    u = u_ref[...]
    cv = _dot(u, wc_ref[...]) * _dot(u, wv_ref[...])
    seq_start = (i * tm) % seq == 0
    for c in range(nb):
        blk = j * nb + c
        cols = slice(c * LANES, (c + 1) * LANES)
        xe_ref[c, pl.ds(0, halo), :] = jnp.where(seq_start, 0.0, hist_ref[blk])
        xe_ref[c, pl.ds(halo, tm), :] = cv[:, cols]
        hist_ref[blk] = cv[tm - halo:, cols]
    gb = _dot(u, wb_ref[...])
    for c in range(nb):
        for r0 in range(0, tm, rows):
            conv = _conv_taps(xe_ref, wconv_ref, c, r0, rows, halo, width)
            gate = gb[r0:r0 + rows, c * LANES:(c + 1) * LANES]
            y_ref[pl.ds(r0, rows), pl.ds(c * LANES, LANES)] = (gate * conv).astype(BF16)


def _mix_b_in(h, g, w_in, w_conv3, *, lg, lw, tm, tn, seq):
    t, d = h.shape
    nj = d // tn
    nb = tn // LANES
    halo = CONV_B_HALO
    kern = functools.partial(_mix_b_in_kernel, tm=tm, seq=seq)
    return pl.pallas_call(
        kern,
        grid=(t // tm, nj),
        in_specs=[
            pl.BlockSpec((tm, d), lambda i, j: (i, 0)),
            pl.BlockSpec((None, 1, d), lambda i, j: (lg, 0, 0)),
            pl.BlockSpec((None, d, tn), lambda i, j: (lw, 0, j)),
            pl.BlockSpec((None, d, tn), lambda i, j: (lw, 0, j + nj)),
            pl.BlockSpec((None, d, tn), lambda i, j: (lw, 0, j + 2 * nj)),
            pl.BlockSpec((None, nb) + w_conv3.shape[2:], lambda i, j: (lw, j, 0, 0)),
        ],
        out_specs=pl.BlockSpec((tm, tn), lambda i, j: (i, j)),
        out_shape=jax.ShapeDtypeStruct((t, d), BF16),
        scratch_shapes=[pltpu.VMEM((tm, d), BF16),
                        pltpu.VMEM((nb, tm + halo, LANES), F32),
                        pltpu.VMEM((d // LANES, halo, LANES), F32)],
        compiler_params=_params(2, ordered_rows=True),
        name="mix_b_in",
    )(h, g, w_in, w_in, w_in, w_conv3)


def _mix_b_out_kernel(y_ref, h_ref, wo_ref, o_ref):
    o_ref[...] = h_ref[...] + _dot(y_ref[...], wo_ref[...])


def _mix_b_out(y, h, w_out, *, lw, tm):
    t, d = h.shape
    return pl.pallas_call(
        _mix_b_out_kernel,
        grid=(t // tm,),
        in_specs=[
            pl.BlockSpec((tm, d), lambda i: (i, 0)),
            pl.BlockSpec((tm, d), lambda i: (i, 0)),
            pl.BlockSpec((None, d, d), lambda i: (lw, 0, 0), pipeline_mode=pl.Buffered(1)),
        ],
        out_specs=pl.BlockSpec((tm, d), lambda i: (i, 0)),
        out_shape=jax.ShapeDtypeStruct((t, d), F32),
        compiler_params=_params(1),
        name="mix_b_out",
    )(y, h, w_out)


def _mlp_kernel(x_ref, g_ref, w1_ref, w2_ref, o_ref, u_ref, *, tn):
    @pl.when(pl.program_id(1) == 0)
    def _():
        x = x_ref[...]
        u_ref[...] = _rmsnorm(x, g_ref[...]).astype(BF16)
        o_ref[...] = x

    hid = jnp.maximum(_dot(u_ref[...], w1_ref[...]), 0.0)
    hid = (hid * hid).astype(BF16)
    for n in range(o_ref.shape[1] // tn):
        cols = pl.ds(n * tn, tn)
        o_ref[:, cols] += _dot(hid, w2_ref[:, cols])


def _mlp(h, g, w1, w2, *, lg, tm, tf):
    t, d = h.shape
    f = w1.shape[2]
    kern = functools.partial(_mlp_kernel, tn=512)
    return pl.pallas_call(
        kern,
        grid=(t // tm, f // tf),
        in_specs=[
            pl.BlockSpec((tm, d), lambda i, j: (i, 0)),
            pl.BlockSpec((None, 1, d), lambda i, j: (lg, 0, 0)),
            pl.BlockSpec((None, d, tf), lambda i, j: (lg, 0, j)),
            pl.BlockSpec((None, tf, d), lambda i, j: (lg, j, 0)),
        ],
        out_specs=pl.BlockSpec((tm, d), lambda i, j: (i, 0)),
        out_shape=jax.ShapeDtypeStruct((t, d), F32),
        scratch_shapes=[pltpu.VMEM((tm, d), BF16)],
        compiler_params=_params(2),
        name="mlp",
    )(h, g, w1, w2)


def _ple_kernel(x_ref, p_ref, g_ref, wg_ref, wp_ref, gf_ref, o_ref, *, close_trunk):
    x = x_ref[...]
    u = _rmsnorm(x, g_ref[...]).astype(BF16)
    gate = jax.nn.sigmoid(_dot(u, wg_ref[...]))
    e = _dot(p_ref[...].astype(BF16), wp_ref[...])
    out = x + gate * e
    o_ref[...] = _rmsnorm(out, gf_ref[...]) if close_trunk else out


def _ple(h, p, g, w_gate, w_proj, g_final, *, lg, tm, close_trunk):
    t, d = h.shape
    k = p.shape[2]
    kern = functools.partial(_ple_kernel, close_trunk=close_trunk)
    return pl.pallas_call(
        kern,
        grid=(t // tm,),
        in_specs=[
            pl.BlockSpec((tm, d), lambda i: (i, 0)),
            pl.BlockSpec((None, tm, k), lambda i: (lg, i, 0)),
            pl.BlockSpec((None, 1, d), lambda i: (lg, 0, 0)),
            pl.BlockSpec((None, d, d), lambda i: (lg, 0, 0), pipeline_mode=pl.Buffered(1)),
            pl.BlockSpec((None, k, d), lambda i: (lg, 0, 0), pipeline_mode=pl.Buffered(1)),
            pl.BlockSpec((1, d), lambda i: (0, 0)),
        ],
        out_specs=pl.BlockSpec((tm, d), lambda i: (i, 0)),
        out_shape=jax.ShapeDtypeStruct((t, d), F32),
        compiler_params=_params(1),
        name="ple",
    )(h, p, g, w_gate, w_proj, g_final)


def _lane_blocks(w, rows):
    n, k, d = w.shape
    w = jnp.pad(w, ((0, 0), (0, rows - k), (0, 0)))
    return w.reshape(n, rows, d // LANES, LANES).transpose(0, 2, 1, 3)


def _replicated_taps(w, taps):
    blocks = _lane_blocks(w, taps).astype(BF16)
    return jnp.broadcast_to(blocks[:, :, :, None, :], blocks.shape[:3] + (PACK, LANES))


def kernel(x, p, norm_mix, norm_mlp, norm_ple, cf_w_pw1, cf_b_pw1, cf_w_dw, cf_b_dw, cf_norm, cf_w_pw2, cf_b_pw2, sc_w_in, sc_w_conv, sc_w_out, mlp_w1, mlp_w2, ple_w_proj, ple_w_gate, norm_final):
    batch, seq, d = x.shape
    depth = p.shape[0]
    t = batch * seq
    h = x.reshape(t, d)
    p = p.reshape(depth, t, p.shape[-1])
    rows = lambda a: a[:, None, :]

    norm_mix, norm_mlp, norm_ple = rows(norm_mix), rows(norm_mlp), rows(norm_ple)
    cf_b_pw1, cf_norm, cf_b_pw2 = rows(cf_b_pw1), rows(cf_norm), rows(cf_b_pw2)
    cf_w_pw1, cf_w_pw2 = cf_w_pw1.astype(BF16), cf_w_pw2.astype(BF16)
    cf_w_rep = _replicated_taps(cf_w_dw, CONV_A_HALO)
    cf_b_dw3 = _lane_blocks(rows(cf_b_dw), 1)
    sc_w_conv3 = _lane_blocks(sc_w_conv, CONV_B_HALO)
    g_final = norm_final.reshape(1, d)

    for i in range(depth):
        j = i // 2
        if i % 2 == 0:
            v3 = _mix_a_in(h, norm_mix, cf_w_pw1, cf_b_pw1, lg=i, lw=j, tm=1024, tn=512)
            h = _mix_a_out(v3, h, cf_w_rep, cf_b_dw3, cf_norm, cf_w_pw2, cf_b_pw2,
                           lw=j, tm=512, seq=seq)
        else:
            y = _mix_b_in(h, norm_mix, sc_w_in, sc_w_conv3, lg=i, lw=j, tm=1024, tn=512, seq=seq)
            h = _mix_b_out(y, h, sc_w_out, lw=j, tm=512)
        h = _mlp(h, norm_mlp, mlp_w1, mlp_w2, lg=i, tm=1024, tf=512)
        h = _ple(h, p, norm_ple, ple_w_gate, ple_w_proj, g_final, lg=i, tm=512,
                 close_trunk=(i == depth - 1))

    return h.reshape(batch, seq, d)
```

```python
import functools

import jax
import jax.numpy as jnp
from jax.experimental import pallas as pl
from jax.experimental.pallas import tpu as pltpu

EPS = 1e-6
LANES = 128
CONV_A_WIDTH = 31
CONV_B_WIDTH = 3
CONV_A_HALO = 32
CONV_B_HALO = 8
CONV_ROW_CHUNK = 64
NORM_ROW_CHUNK = 128
SHORT_CONV_ROW_CHUNK = 256
PROJ_COLS = 256
V7X_VMEM_LIMIT_BYTES = 56 * 1024 * 1024

F32 = jnp.float32
BF16 = jnp.bfloat16


def _params(n_axes, ordered_rows=False):
    sem = ("arbitrary" if ordered_rows else "parallel",) + ("arbitrary",) * (n_axes - 1)
    return pltpu.CompilerParams(dimension_semantics=sem,
                                vmem_limit_bytes=V7X_VMEM_LIMIT_BYTES)


def _rmsnorm(x, g):
    r = jax.lax.rsqrt(jnp.mean(x * x, axis=-1, keepdims=True) + EPS)
    return (x * r) * g


def _dot(a, b):
    return jnp.dot(a, b.astype(BF16), preferred_element_type=F32)


def _conv_taps(xe_ref, w_ref, cb, r0, rows, halo, width):
    base = halo - (width - 1)
    acc = None
    for k in range(width):
        term = xe_ref[cb, pl.ds(r0 + base + k, rows), :] * w_ref[cb, pl.ds(k, 1), :]
        acc = term if acc is None else acc + term
    return acc


def _mix_a_kernel(x_ref, g_ref, w1_ref, b1_ref, wdw_ref, bdw_ref, gn_ref, y_ref,
                  u_ref, xe_ref, hist_ref, c_ref, *, tm, seq):
    halo, width, rows = CONV_A_HALO, CONV_A_WIDTH, CONV_ROW_CHUNK
    i = pl.program_id(0)
    d = x_ref.shape[1]
    nblk = d // LANES
    pb = PROJ_COLS // LANES

    @pl.when(i == 0)
    def _():
        hist_ref[...] = jnp.zeros(hist_ref.shape, F32)

    u_ref[...] = _rmsnorm(x_ref[...], g_ref[...]).astype(BF16)
    u = u_ref[...]
    seq_start = (i * tm) % seq == 0
    for s in range(d // PROJ_COLS):
        vcols = pl.ds(s * PROJ_COLS, PROJ_COLS)
        gcols = pl.ds(d + s * PROJ_COLS, PROJ_COLS)
        val = _dot(u, w1_ref[:, vcols]) + b1_ref[:, vcols]
        gate = _dot(u, w1_ref[:, gcols]) + b1_ref[:, gcols]
        v = val * jax.nn.sigmoid(gate)
        for c in range(s * pb, (s + 1) * pb):
            vc = v[:, (c - s * pb) * LANES:(c - s * pb + 1) * LANES]
            xe_ref[c, pl.ds(0, halo), :] = jnp.where(seq_start, 0.0, hist_ref[c])
            xe_ref[c, pl.ds(halo, tm), :] = vc
            hist_ref[c] = vc[tm - halo:, :]
            for r0 in range(0, tm, rows):
                acc = _conv_taps(xe_ref, wdw_ref, c, r0, rows, halo, width) + bdw_ref[c]
                c_ref[c, pl.ds(r0, rows), :] = acc

    for r0 in range(0, tm, NORM_ROW_CHUNK):
        rsl = pl.ds(r0, NORM_ROW_CHUNK)
        ssq = None
        for cb in range(nblk):
            cc = c_ref[cb, rsl, :]
            ssq = cc * cc if ssq is None else ssq + cc * cc
        r = jax.lax.rsqrt(jnp.sum(ssq, axis=-1, keepdims=True) * (1.0 / d) + EPS)
        r = jnp.broadcast_to(r, (NORM_ROW_CHUNK, LANES))
        for cb in range(nblk):
            cols = pl.ds(cb * LANES, LANES)
            y = (c_ref[cb, rsl, :] * r) * gn_ref[:, cols]
            y_ref[rsl, cols] = (y * jax.nn.sigmoid(y)).astype(BF16)


def _mix_a(h, g, w1, b1, w_dw3, b_dw3, g_norm, *, lg, lw, tm, seq):
    t, d = h.shape
    nblk = d // LANES
    halo = CONV_A_HALO
    kern = functools.partial(_mix_a_kernel, tm=tm, seq=seq)
    return pl.pallas_call(
        kern,
        grid=(t // tm,),
        in_specs=[
            pl.BlockSpec((tm, d), lambda i: (i, 0)),
            pl.BlockSpec((None, 1, d), lambda i: (lg, 0, 0)),
            pl.BlockSpec((None, d, 2 * d), lambda i: (lw, 0, 0), pipeline_mode=pl.Buffered(1)),
            pl.BlockSpec((None, 1, 2 * d), lambda i: (lw, 0, 0)),
            pl.BlockSpec((None,) + w_dw3.shape[1:], lambda i: (lw, 0, 0, 0)),
            pl.BlockSpec((None, nblk, 1, LANES), lambda i: (lw, 0, 0, 0)),
            pl.BlockSpec((None, 1, d), lambda i: (lw, 0, 0)),
        ],
        out_specs=pl.BlockSpec((tm, d), lambda i: (i, 0)),
        out_shape=jax.ShapeDtypeStruct((t, d), BF16),
        scratch_shapes=[pltpu.VMEM((tm, d), BF16),
                        pltpu.VMEM((nblk, halo + tm, LANES), F32),
                        pltpu.VMEM((nblk, halo, LANES), F32),
                        pltpu.VMEM((nblk, tm, LANES), F32)],
        compiler_params=_params(1, ordered_rows=True),
        name="mix_a",
    )(h, g, w1, b1, w_dw3, b_dw3, g_norm)


def _mix_b_in_kernel(x_ref, g_ref, wb_ref, wc_ref, wv_ref, wconv_ref, y_ref,
                     u_ref, xe_ref, hist_ref, *, tm, seq):
    halo, width, rows = CONV_B_HALO, CONV_B_WIDTH, SHORT_CONV_ROW_CHUNK
    i, j = pl.program_id(0), pl.program_id(1)
    nb = xe_ref.shape[0]

    @pl.when(j == 0)
    def _():
        u_ref[...] = _rmsnorm(x_ref[...], g_ref[...]).astype(BF16)

    @pl.when(i == 0)
    def _():
        for c in range(nb):
            hist_ref[j * nb + c] = jnp.zeros((halo, LANES), F32)

    u = u_ref[...]
    cv = _dot(u, wc_ref[...]) * _dot(u, wv_ref[...])
    seq_start = (i * tm) % seq == 0
    for c in range(nb):
        blk = j * nb + c
        cols = slice(c * LANES, (c + 1) * LANES)
        xe_ref[c, pl.ds(0, halo), :] = jnp.where(seq_start, 0.0, hist_ref[blk])
        xe_ref[c, pl.ds(halo, tm), :] = cv[:, cols]
        hist_ref[blk] = cv[tm - halo:, cols]
    gb = _dot(u, wb_ref[...])
    for c in range(nb):
        for r0 in range(0, tm, rows):
            conv = _conv_taps(xe_ref, wconv_ref, c, r0, rows, halo, width)
            gate = gb[r0:r0 + rows, c * LANES:(c + 1) * LANES]
            y_ref[pl.ds(r0, rows), pl.ds(c * LANES, LANES)] = (gate * conv).astype(BF16)


def _mix_b_in(h, g, w_in, w_conv3, *, lg, lw, tm, tn, seq):
    t, d = h.shape
    nj = d // tn
    nb = tn // LANES
    halo = CONV_B_HALO
    kern = functools.partial(_mix_b_in_kernel, tm=tm, seq=seq)
    return pl.pallas_call(
        kern,
        grid=(t // tm, nj),
        in_specs=[
            pl.BlockSpec((tm, d), lambda i, j: (i, 0)),
            pl.BlockSpec((None, 1, d), lambda i, j: (lg, 0, 0)),
            pl.BlockSpec((None, d, tn), lambda i, j: (lw, 0, j)),
            pl.BlockSpec((None, d, tn), lambda i, j: (lw, 0, j + nj)),
            pl.BlockSpec((None, d, tn), lambda i, j: (lw, 0, j + 2 * nj)),
            pl.BlockSpec((None, nb) + w_conv3.shape[2:], lambda i, j: (lw, j, 0, 0)),
        ],
        out_specs=pl.BlockSpec((tm, tn), lambda i, j: (i, j)),
        out_shape=jax.ShapeDtypeStruct((t, d), BF16),
        scratch_shapes=[pltpu.VMEM((tm, d), BF16),
                        pltpu.VMEM((nb, tm + halo, LANES), F32),
                        pltpu.VMEM((d // LANES, halo, LANES), F32)],
        compiler_params=_params(2, ordered_rows=True),
        name="mix_b_in",
    )(h, g, w_in, w_in, w_in, w_conv3)


def _mix_out_kernel(y_ref, h_ref, w_ref, b_ref, o_ref):
    o_ref[...] = h_ref[...] + _dot(y_ref[...], w_ref[...]) + b_ref[...]


def _mix_out(y, h, w, b, *, lw, lb, tm):
    t, d = h.shape
    return pl.pallas_call(
        _mix_out_kernel,
        grid=(t // tm,),
        in_specs=[
            pl.BlockSpec((tm, d), lambda i: (i, 0)),
            pl.BlockSpec((tm, d), lambda i: (i, 0)),
            pl.BlockSpec((None, d, d), lambda i: (lw, 0, 0), pipeline_mode=pl.Buffered(1)),
            pl.BlockSpec((None, 1, d), lambda i: (lb, 0, 0)),
        ],
        out_specs=pl.BlockSpec((tm, d), lambda i: (i, 0)),
        out_shape=jax.ShapeDtypeStruct((t, d), F32),
        compiler_params=_params(1),
        name="mix_out",
    )(y, h, w, b)


def _mlp_kernel(x_ref, g_ref, w1_ref, w2_ref, o_ref, u_ref, *, tn):
    @pl.when(pl.program_id(1) == 0)
    def _():
        x = x_ref[...]
        u_ref[...] = _rmsnorm(x, g_ref[...]).astype(BF16)
        o_ref[...] = x

    hid = jnp.maximum(_dot(u_ref[...], w1_ref[...]), 0.0)
    hid = (hid * hid).astype(BF16)
    for n in range(o_ref.shape[1] // tn):
        cols = pl.ds(n * tn, tn)
        o_ref[:, cols] += _dot(hid, w2_ref[:, cols])


def _mlp(h, g, w1, w2, *, lg, tm, tf):
    t, d = h.shape
    f = w1.shape[2]
    kern = functools.partial(_mlp_kernel, tn=512)
    return pl.pallas_call(
        kern,
        grid=(t // tm, f // tf),
        in_specs=[
            pl.BlockSpec((tm, d), lambda i, j: (i, 0)),
            pl.BlockSpec((None, 1, d), lambda i, j: (lg, 0, 0)),
            pl.BlockSpec((None, d, tf), lambda i, j: (lg, 0, j)),
            pl.BlockSpec((None, tf, d), lambda i, j: (lg, j, 0)),
        ],
        out_specs=pl.BlockSpec((tm, d), lambda i, j: (i, 0)),
        out_shape=jax.ShapeDtypeStruct((t, d), F32),
        scratch_shapes=[pltpu.VMEM((tm, d), BF16)],
        compiler_params=_params(2),
        name="mlp",
    )(h, g, w1, w2)


def _ple_kernel(x_ref, p_ref, g_ref, wg_ref, wp_ref, gf_ref, o_ref, *, close_trunk):
    x = x_ref[...]
    u = _rmsnorm(x, g_ref[...]).astype(BF16)
    gate = jax.nn.sigmoid(_dot(u, wg_ref[...]))
    e = _dot(p_ref[...].astype(BF16), wp_ref[...])
    out = x + gate * e
    o_ref[...] = _rmsnorm(out, gf_ref[...]) if close_trunk else out


def _ple(h, p, g, w_gate, w_proj, g_final, *, lg, tm, close_trunk):
    t, d = h.shape
    k = p.shape[2]
    kern = functools.partial(_ple_kernel, close_trunk=close_trunk)
    return pl.pallas_call(
        kern,
        grid=(t // tm,),
        in_specs=[
            pl.BlockSpec((tm, d), lambda i: (i, 0)),
            pl.BlockSpec((None, tm, k), lambda i: (lg, i, 0)),
            pl.BlockSpec((None, 1, d), lambda i: (lg, 0, 0)),
            pl.BlockSpec((None, d, d), lambda i: (lg, 0, 0), pipeline_mode=pl.Buffered(1)),
            pl.BlockSpec((None, k, d), lambda i: (lg, 0, 0), pipeline_mode=pl.Buffered(1)),
            pl.BlockSpec((1, d), lambda i: (0, 0)),
        ],
        out_specs=pl.BlockSpec((tm, d), lambda i: (i, 0)),
        out_shape=jax.ShapeDtypeStruct((t, d), F32),
        compiler_params=_params(1),
        name="ple",
    )(h, p, g, w_gate, w_proj, g_final)


def _lane_blocks(w, rows):
    n, k, d = w.shape
    w = jnp.pad(w, ((0, 0), (0, rows - k), (0, 0)))
    return w.reshape(n, rows, d // LANES, LANES).transpose(0, 2, 1, 3)


def kernel(x, p, norm_mix, norm_mlp, norm_ple, cf_w_pw1, cf_b_pw1, cf_w_dw, cf_b_dw, cf_norm, cf_w_pw2, cf_b_pw2, sc_w_in, sc_w_conv, sc_w_out, mlp_w1, mlp_w2, ple_w_proj, ple_w_gate, norm_final):
    batch, seq, d = x.shape
    depth = p.shape[0]
    t = batch * seq
    h = x.reshape(t, d)
    p = p.reshape(depth, t, p.shape[-1])
    rows = lambda a: a[:, None, :]

    norm_mix, norm_mlp, norm_ple = rows(norm_mix), rows(norm_mlp), rows(norm_ple)
    cf_b_pw1, cf_norm, cf_b_pw2 = rows(cf_b_pw1), rows(cf_norm), rows(cf_b_pw2)
    cf_w_pw1 = cf_w_pw1.astype(BF16)
    no_bias = jnp.zeros((1, 1, d), F32)
    cf_w_dw3 = _lane_blocks(cf_w_dw, CONV_A_HALO)
    cf_b_dw3 = _lane_blocks(rows(cf_b_dw), 1)
    sc_w_conv3 = _lane_blocks(sc_w_conv, CONV_B_HALO)
    g_final = norm_final.reshape(1, d)

    for i in range(depth):
        j = i // 2
        if i % 2 == 0:
            y = _mix_a(h, norm_mix, cf_w_pw1, cf_b_pw1, cf_w_dw3, cf_b_dw3, cf_norm,
                       lg=i, lw=j, tm=512, seq=seq)
            h = _mix_out(y, h, cf_w_pw2, cf_b_pw2, lw=j, lb=j, tm=512)
        else:
            y = _mix_b_in(h, norm_mix, sc_w_in, sc_w_conv3, lg=i, lw=j, tm=1024, tn=512, seq=seq)
            h = _mix_out(y, h, sc_w_out, no_bias, lw=j, lb=0, tm=512)
        h = _mlp(h, norm_mlp, mlp_w1, mlp_w2, lg=i, tm=1024, tf=512)
        h = _ple(h, p, norm_ple, ple_w_gate, ple_w_proj, g_final, lg=i, tm=512,
                 close_trunk=(i == depth - 1))

    return h.reshape(batch, seq, d)
```

```python
import functools

import jax
import jax.numpy as jnp
from jax.experimental import pallas as pl
from jax.experimental.pallas import tpu as pltpu

EPS = 1e-6
LANES = 128
CONV_A_WIDTH = 31
CONV_B_WIDTH = 3
CONV_A_HALO = 32
CONV_B_HALO = 8
CONV_ROW_CHUNK = 64
NORM_ROW_CHUNK = 128
SHORT_CONV_ROW_CHUNK = 256
OUT_PROJ_AFTER_CHUNK = (5, 7)
PROJ_COLS = 256
V7X_VMEM_LIMIT_BYTES = 56 * 1024 * 1024
MIX_A_VMEM_LIMIT_BYTES = 60 * 1024 * 1024

F32 = jnp.float32
BF16 = jnp.bfloat16


def _params(n_axes, ordered_rows=False):
    sem = ("arbitrary" if ordered_rows else "parallel",) + ("arbitrary",) * (n_axes - 1)
    return pltpu.CompilerParams(dimension_semantics=sem,
                                vmem_limit_bytes=V7X_VMEM_LIMIT_BYTES)


def _rmsnorm(x, g):
    r = jax.lax.rsqrt(jnp.mean(x * x, axis=-1, keepdims=True) + EPS)
    return (x * r) * g


def _dot(a, b):
    return jnp.dot(a, b.astype(BF16), preferred_element_type=F32)


def _conv_taps(xe_ref, w_ref, cb, r0, rows, halo, width):
    base = halo - (width - 1)
    acc = None
    for k in range(width):
        term = xe_ref[cb, pl.ds(r0 + base + k, rows), :] * w_ref[cb, pl.ds(k, 1), :]
        acc = term if acc is None else acc + term
    return acc


def _mix_a_kernel(x_ref, g_ref, w1_ref, b1_ref, wdw_ref, bdw_ref, gn_ref, w2_ref, b2_ref, o_ref,
                  u_ref, xe_ref, hist_ref, c_ref, y_ref, xprev_ref, *, tm, seq):
    halo, width, rows = CONV_A_HALO, CONV_A_WIDTH, CONV_ROW_CHUNK
    i = pl.program_id(0)
    d = x_ref.shape[1]
    nblk = d // LANES
    pb = PROJ_COLS // LANES

    @pl.when(i == 0)
    def _():
        hist_ref[...] = jnp.zeros(hist_ref.shape, F32)
        y_ref[...] = jnp.zeros(y_ref.shape, BF16)
        xprev_ref[...] = jnp.zeros(xprev_ref.shape, F32)

    u_ref[...] = _rmsnorm(x_ref[...], g_ref[...]).astype(BF16)
    u = u_ref[...]
    seq_start = (i * tm) % seq == 0
    for s in range(d // PROJ_COLS):
        vcols = pl.ds(s * PROJ_COLS, PROJ_COLS)
        gcols = pl.ds(d + s * PROJ_COLS, PROJ_COLS)
        val = _dot(u, w1_ref[:, vcols]) + b1_ref[:, vcols]
        gate = _dot(u, w1_ref[:, gcols]) + b1_ref[:, gcols]
        if s in OUT_PROJ_AFTER_CHUNK:
            ocols = pl.ds(OUT_PROJ_AFTER_CHUNK.index(s) * (d // 2), d // 2)
            o_ref[:, ocols] = xprev_ref[:, ocols] + _dot(y_ref[...], w2_ref[:, ocols]) + b2_ref[:, ocols]
        v = val * jax.nn.sigmoid(gate)
        for c in range(s * pb, (s + 1) * pb):
            vc = v[:, (c - s * pb) * LANES:(c - s * pb + 1) * LANES]
            xe_ref[c, pl.ds(0, halo), :] = jnp.where(seq_start, 0.0, hist_ref[c])
            xe_ref[c, pl.ds(halo, tm), :] = vc
            hist_ref[c] = vc[tm - halo:, :]
            for r0 in range(0, tm, rows):
                acc = _conv_taps(xe_ref, wdw_ref, c, r0, rows, halo, width) + bdw_ref[c]
                c_ref[c, pl.ds(r0, rows), :] = acc

    xprev_ref[...] = x_ref[...]

    for r0 in range(0, tm, NORM_ROW_CHUNK):
        rsl = pl.ds(r0, NORM_ROW_CHUNK)
        ssq = None
        for cb in range(nblk):
            cc = c_ref[cb, rsl, :]
            ssq = cc * cc if ssq is None else ssq + cc * cc
        r = jax.lax.rsqrt(jnp.sum(ssq, axis=-1, keepdims=True) * (1.0 / d) + EPS)
        r = jnp.broadcast_to(r, (NORM_ROW_CHUNK, LANES))
        for cb in range(nblk):
            cols = pl.ds(cb * LANES, LANES)
            y = (c_ref[cb, rsl, :] * r) * gn_ref[:, cols]
            y_ref[rsl, cols] = (y * jax.nn.sigmoid(y)).astype(BF16)


def _mix_a(h, g, w1, b1, w_dw3, b_dw3, g_norm, w2, b2, *, lg, lw, tm, seq):
    t, d = h.shape
    nblk = d // LANES
    halo = CONV_A_HALO
    n_tiles = t // tm
    kern = functools.partial(_mix_a_kernel, tm=tm, seq=seq)
    return pl.pallas_call(
        kern,
        grid=(n_tiles + 1,),
        in_specs=[
            pl.BlockSpec((tm, d), lambda i: (jnp.minimum(i, n_tiles - 1), 0)),
            pl.BlockSpec((None, 1, d), lambda i: (lg, 0, 0)),
            pl.BlockSpec((None, d, 2 * d), lambda i: (lw, 0, 0), pipeline_mode=pl.Buffered(1)),
            pl.BlockSpec((None, 1, 2 * d), lambda i: (lw, 0, 0)),
            pl.BlockSpec((None,) + w_dw3.shape[1:], lambda i: (lw, 0, 0, 0)),
            pl.BlockSpec((None, nblk, 1, LANES), lambda i: (lw, 0, 0, 0)),
            pl.BlockSpec((None, 1, d), lambda i: (lw, 0, 0)),
            pl.BlockSpec((None, d, d), lambda i: (lw, 0, 0), pipeline_mode=pl.Buffered(1)),
            pl.BlockSpec((None, 1, d), lambda i: (lw, 0, 0)),
        ],
        out_specs=pl.BlockSpec((tm, d), lambda i: (jnp.maximum(i - 1, 0), 0)),
        out_shape=jax.ShapeDtypeStruct((t, d), F32),
        scratch_shapes=[pltpu.VMEM((tm, d), BF16),
                        pltpu.VMEM((nblk, halo + tm, LANES), F32),
                        pltpu.VMEM((nblk, halo, LANES), F32),
                        pltpu.VMEM((nblk, tm, LANES), F32),
                        pltpu.VMEM((tm, d), BF16),
                        pltpu.VMEM((tm, d), F32)],
        compiler_params=pltpu.CompilerParams(dimension_semantics=("arbitrary",),
                                             vmem_limit_bytes=MIX_A_VMEM_LIMIT_BYTES),
        name="mix_a",
    )(h, g, w1, b1, w_dw3, b_dw3, g_norm, w2, b2)


def _mix_b_in_kernel(x_ref, g_ref, wb_ref, wc_ref, wv_ref, wconv_ref, y_ref,
                     u_ref, xe_ref, hist_ref, *, tm, seq):
    halo, width, rows = CONV_B_HALO, CONV_B_WIDTH, SHORT_CONV_ROW_CHUNK
    i, j = pl.program_id(0), pl.program_id(1)
    nb = xe_ref.shape[0]

    @pl.when(j == 0)
    def _():
        u_ref[...] = _rmsnorm(x_ref[...], g_ref[...]).astype(BF16)

    @pl.when(i == 0)
    def _():
        for c in range(nb):
            hist_ref[j * nb + c] = jnp.zeros((halo, LANES), F32)

    u = u_ref[...]
    cv = _dot(u, wc_ref[...]) * _dot(u, wv_ref[...])
    seq_start = (i * tm) % seq == 0
    for c in range(nb):
        blk = j * nb + c
        cols = slice(c * LANES, (c + 1) * LANES)
        xe_ref[c, pl.ds(0, halo), :] = jnp.where(seq_start, 0.0, hist_ref[blk])
        xe_ref[c, pl.ds(halo, tm), :] = cv[:, cols]
        hist_ref[blk] = cv[tm - halo:, cols]
    gb = _dot(u, wb_ref[...])
    for c in range(nb):
        for r0 in range(0, tm, rows):
            conv = _conv_taps(xe_ref, wconv_ref, c, r0, rows, halo, width)
            gate = gb[r0:r0 + rows, c * LANES:(c + 1) * LANES]
            y_ref[pl.ds(r0, rows), pl.ds(c * LANES, LANES)] = (gate * conv).astype(BF16)


def _mix_b_in(h, g, w_in, w_conv3, *, lg, lw, tm, tn, seq):
    t, d = h.shape
    nj = d // tn
    nb = tn // LANES
    halo = CONV_B_HALO
    kern = functools.partial(_mix_b_in_kernel, tm=tm, seq=seq)
    return pl.pallas_call(
        kern,
        grid=(t // tm, nj),
        in_specs=[
            pl.BlockSpec((tm, d), lambda i, j: (i, 0)),
            pl.BlockSpec((None, 1, d), lambda i, j: (lg, 0, 0)),
            pl.BlockSpec((None, d, tn), lambda i, j: (lw, 0, j)),
            pl.BlockSpec((None, d, tn), lambda i, j: (lw, 0, j + nj)),
            pl.BlockSpec((None, d, tn), lambda i, j: (lw, 0, j + 2 * nj)),
            pl.BlockSpec((None, nb) + w_conv3.shape[2:], lambda i, j: (lw, j, 0, 0)),
        ],
        out_specs=pl.BlockSpec((tm, tn), lambda i, j: (i, j)),
        out_shape=jax.ShapeDtypeStruct((t, d), BF16),
        scratch_shapes=[pltpu.VMEM((tm, d), BF16),
                        pltpu.VMEM((nb, tm + halo, LANES), F32),
                        pltpu.VMEM((d // LANES, halo, LANES), F32)],
        compiler_params=_params(2, ordered_rows=True),
        name="mix_b_in",
    )(h, g, w_in, w_in, w_in, w_conv3)


def _mix_out_kernel(y_ref, h_ref, w_ref, b_ref, o_ref):
    o_ref[...] = h_ref[...] + _dot(y_ref[...], w_ref[...]) + b_ref[...]


def _mix_out(y, h, w, b, *, lw, lb, tm):
    t, d = h.shape
    return pl.pallas_call(
        _mix_out_kernel,
        grid=(t // tm,),
        in_specs=[
            pl.BlockSpec((tm, d), lambda i: (i, 0)),
            pl.BlockSpec((tm, d), lambda i: (i, 0)),
            pl.BlockSpec((None, d, d), lambda i: (lw, 0, 0), pipeline_mode=pl.Buffered(1)),
            pl.BlockSpec((None, 1, d), lambda i: (lb, 0, 0)),
        ],
        out_specs=pl.BlockSpec((tm, d), lambda i: (i, 0)),
        out_shape=jax.ShapeDtypeStruct((t, d), F32),
        compiler_params=_params(1),
        name="mix_out",
    )(y, h, w, b)


def _mlp_kernel(x_ref, g_ref, w1_ref, w2_ref, o_ref, u_ref, *, tn):
    @pl.when(pl.program_id(1) == 0)
    def _():
        x = x_ref[...]
        u_ref[...] = _rmsnorm(x, g_ref[...]).astype(BF16)
        o_ref[...] = x

    hid = jnp.maximum(_dot(u_ref[...], w1_ref[...]), 0.0)
    hid = (hid * hid).astype(BF16)
    for n in range(o_ref.shape[1] // tn):
        cols = pl.ds(n * tn, tn)
        o_ref[:, cols] += _dot(hid, w2_ref[:, cols])


def _mlp(h, g, w1, w2, *, lg, tm, tf):
    t, d = h.shape
    f = w1.shape[2]
    kern = functools.partial(_mlp_kernel, tn=512)
    return pl.pallas_call(
        kern,
        grid=(t // tm, f // tf),
        in_specs=[
            pl.BlockSpec((tm, d), lambda i, j: (i, 0)),
            pl.BlockSpec((None, 1, d), lambda i, j: (lg, 0, 0)),
            pl.BlockSpec((None, d, tf), lambda i, j: (lg, 0, j)),
            pl.BlockSpec((None, tf, d), lambda i, j: (lg, j, 0)),
        ],
        out_specs=pl.BlockSpec((tm, d), lambda i, j: (i, 0)),
        out_shape=jax.ShapeDtypeStruct((t, d), F32),
        scratch_shapes=[pltpu.VMEM((tm, d), BF16)],
        compiler_params=_params(2),
        name="mlp",
    )(h, g, w1, w2)


def _ple_kernel(x_ref, p_ref, g_ref, wg_ref, wp_ref, gf_ref, o_ref, *, close_trunk):
    x = x_ref[...]
    u = _rmsnorm(x, g_ref[...]).astype(BF16)
    gate = jax.nn.sigmoid(_dot(u, wg_ref[...]))
    e = _dot(p_ref[...].astype(BF16), wp_ref[...])
    out = x + gate * e
    o_ref[...] = _rmsnorm(out, gf_ref[...]) if close_trunk else out


def _ple(h, p, g, w_gate, w_proj, g_final, *, lg, tm, close_trunk):
    t, d = h.shape
    k = p.shape[2]
    kern = functools.partial(_ple_kernel, close_trunk=close_trunk)
    return pl.pallas_call(
        kern,
        grid=(t // tm,),
        in_specs=[
            pl.BlockSpec((tm, d), lambda i: (i, 0)),
            pl.BlockSpec((None, tm, k), lambda i: (lg, i, 0)),
            pl.BlockSpec((None, 1, d), lambda i: (lg, 0, 0)),
            pl.BlockSpec((None, d, d), lambda i: (lg, 0, 0), pipeline_mode=pl.Buffered(1)),
            pl.BlockSpec((None, k, d), lambda i: (lg, 0, 0), pipeline_mode=pl.Buffered(1)),
            pl.BlockSpec((1, d), lambda i: (0, 0)),
        ],
        out_specs=pl.BlockSpec((tm, d), lambda i: (i, 0)),
        out_shape=jax.ShapeDtypeStruct((t, d), F32),
        compiler_params=_params(1),
        name="ple",
    )(h, p, g, w_gate, w_proj, g_final)


def _lane_blocks(w, rows):
    n, k, d = w.shape
    w = jnp.pad(w, ((0, 0), (0, rows - k), (0, 0)))
    return w.reshape(n, rows, d // LANES, LANES).transpose(0, 2, 1, 3)


def kernel(x, p, norm_mix, norm_mlp, norm_ple, cf_w_pw1, cf_b_pw1, cf_w_dw, cf_b_dw, cf_norm, cf_w_pw2, cf_b_pw2, sc_w_in, sc_w_conv, sc_w_out, mlp_w1, mlp_w2, ple_w_proj, ple_w_gate, norm_final):
    batch, seq, d = x.shape
    depth = p.shape[0]
    t = batch * seq
    h = x.reshape(t, d)
    p = p.reshape(depth, t, p.shape[-1])
    rows = lambda a: a[:, None, :]

    norm_mix, norm_mlp, norm_ple = rows(norm_mix), rows(norm_mlp), rows(norm_ple)
    cf_b_pw1, cf_norm, cf_b_pw2 = rows(cf_b_pw1), rows(cf_norm), rows(cf_b_pw2)
    cf_w_pw1, cf_w_pw2 = cf_w_pw1.astype(BF16), cf_w_pw2.astype(BF16)
    no_bias = jnp.zeros((1, 1, d), F32)
    cf_w_dw3 = _lane_blocks(cf_w_dw, CONV_A_HALO)
    cf_b_dw3 = _lane_blocks(rows(cf_b_dw), 1)
    sc_w_conv3 = _lane_blocks(sc_w_conv, CONV_B_HALO)
    g_final = norm_final.reshape(1, d)

    for i in range(depth):
        j = i // 2
        if i % 2 == 0:
            h = _mix_a(h, norm_mix, cf_w_pw1, cf_b_pw1, cf_w_dw3, cf_b_dw3, cf_norm,
                       cf_w_pw2, cf_b_pw2, lg=i, lw=j, tm=512, seq=seq)
        else:
            y = _mix_b_in(h, norm_mix, sc_w_in, sc_w_conv3, lg=i, lw=j, tm=1024, tn=512, seq=seq)
            h = _mix_out(y, h, sc_w_out, no_bias, lw=j, lb=0, tm=512)
        h = _mlp(h, norm_mlp, mlp_w1, mlp_w2, lg=i, tm=1024, tf=512)
        h = _ple(h, p, norm_ple, ple_w_gate, ple_w_proj, g_final, lg=i, tm=512,
                 close_trunk=(i == depth - 1))

    return h.reshape(batch, seq, d)
```
